```python
import math
import jax, jax.numpy as jnp
from jax import lax
import numpy as np

D_MODEL = 1024
BATCH = 4
SEQ = 8192
DEPTH = 2

HEAD_DIM = 64
N_Q_HEADS = 8
N_KV_HEADS = 2
GQA_GROUP = N_Q_HEADS // N_KV_HEADS
ATTN_WIDTH = N_Q_HEADS * HEAD_DIM
KV_WIDTH = N_KV_HEADS * HEAD_DIM
N_CONV_GROUPS = 8
CONV_WIDTH = N_CONV_GROUPS * HEAD_DIM
MIX_WIDTH = ATTN_WIDTH + CONV_WIDTH
CONV_K = 3
WINDOW = 128
BLOCK = 128
D_FF = 2816
NORM_EPS = 1e-6

Q_END = ATTN_WIDTH
K_END = Q_END + KV_WIDTH
V_END = K_END + KV_WIDTH
CB_END = V_END + CONV_WIDTH
CC_END = CB_END + CONV_WIDTH
CH_END = CC_END + CONV_WIDTH
IN_WIDTH = CH_END

kernel_name = "hybrid_macaron_swa_shortconv"


def rms_norm(x, gain):
    xf = x.astype(jnp.float32)
    inv = lax.rsqrt(jnp.mean(xf * xf, axis=-1, keepdims=True) + NORM_EPS)
    return (xf * inv).astype(x.dtype) * gain


def swiglu(h, w_gate, w_up, w_down):
    return (jax.nn.silu(h @ w_gate) * (h @ w_up)) @ w_down


def alibi_slopes(n_heads):
    return jnp.exp2(-8.0 * jnp.arange(1, n_heads + 1, dtype=jnp.float32) / n_heads)


def short_conv(b_gate, c_gate, h_conv, conv_w):
    u = c_gate * h_conv
    w = conv_w.astype(u.dtype)[:, None, :]
    y = lax.conv_general_dilated(
        u, w, window_strides=(1,), padding=[(CONV_K - 1, 0)],
        dimension_numbers=("NWC", "WIO", "NWC"), feature_group_count=CONV_WIDTH)
    return b_gate * y


def sliding_window_attention(q, k, v, sink):
    bsz, seq = q.shape[0], q.shape[1]
    nb = seq // BLOCK
    qb = q.reshape(bsz, nb, BLOCK, N_KV_HEADS, GQA_GROUP, HEAD_DIM)
    kb = k.reshape(bsz, nb, BLOCK, N_KV_HEADS, HEAD_DIM)
    vb = v.reshape(bsz, nb, BLOCK, N_KV_HEADS, HEAD_DIM)
    pad = ((0, 0), (1, 0), (0, 0), (0, 0), (0, 0))
    k_band = jnp.concatenate([jnp.pad(kb, pad)[:, :-1], kb], axis=2)
    v_band = jnp.concatenate([jnp.pad(vb, pad)[:, :-1], vb], axis=2)

    scale = 1.0 / math.sqrt(HEAD_DIM)
    scores = jnp.einsum("bnqkgd,bnskd->bnkgqs", qb, k_band).astype(jnp.float32) * scale

    qi = jnp.arange(BLOCK)[:, None]
    kj = jnp.arange(2 * BLOCK)[None, :]
    dist = qi + BLOCK - kj
    in_window = (dist >= 0) & (dist < WINDOW)
    blk = jnp.arange(nb)[:, None, None]
    mask = in_window[None] & ((blk > 0) | (kj[None] >= BLOCK))

    slopes = alibi_slopes(N_Q_HEADS).reshape(N_KV_HEADS, GQA_GROUP)
    bias = -slopes[:, :, None, None] * dist.astype(jnp.float32)[None, None]
    scores = scores + bias[None, None]
    mask_b = mask[None, :, None, None]
    scores = jnp.where(mask_b, scores, -jnp.inf)

    sink_l = sink.astype(jnp.float32).reshape(1, 1, N_KV_HEADS, GQA_GROUP, 1, 1)
    m = jnp.maximum(jnp.max(scores, axis=-1, keepdims=True), sink_l)
    p = jnp.where(mask_b, jnp.exp(scores - m), 0.0)
    denom = jnp.sum(p, axis=-1, keepdims=True) + jnp.exp(sink_l - m)
    probs = (p / denom).astype(v.dtype)
    out = jnp.einsum("bnkgqs,bnskd->bnqkgd", probs, v_band)
    return out.reshape(bsz, seq, ATTN_WIDTH)


def setup_inputs(seed: int = 0) -> dict:
    key = jax.random.key(seed)
    ks = jax.random.split(key, 20)
    f32 = jnp.float32

    def w(k, shape, fan_in):
        return jax.random.normal(k, shape, f32) * (fan_in ** -0.5)

    def gain(k, shape):
        return 1.0 + 0.05 * jax.random.normal(k, shape, f32)

    return {
        "x": jax.random.normal(ks[0], (BATCH, SEQ, D_MODEL), f32),
        "ffn1_norm": gain(ks[1], (DEPTH, D_MODEL)),
        "ffn1_wg": w(ks[2], (DEPTH, D_MODEL, D_FF), D_MODEL),
        "ffn1_wu": w(ks[3], (DEPTH, D_MODEL, D_FF), D_MODEL),
        "ffn1_wd": w(ks[4], (DEPTH, D_FF, D_MODEL), D_FF),
        "mix_norm": gain(ks[5], (DEPTH, D_MODEL)),
        "w_in": w(ks[6], (DEPTH, D_MODEL, IN_WIDTH), D_MODEL),
        "conv_w": w(ks[7], (DEPTH, CONV_K, CONV_WIDTH), CONV_K),
        "attn_sink": 0.5 * jax.random.normal(ks[8], (DEPTH, N_Q_HEADS), f32),
        "w_out": w(ks[9], (DEPTH, MIX_WIDTH, D_MODEL), MIX_WIDTH),
        "ffn2_norm": gain(ks[10], (DEPTH, D_MODEL)),
        "ffn2_wg": w(ks[11], (DEPTH, D_MODEL, D_FF), D_MODEL),
        "ffn2_wu": w(ks[12], (DEPTH, D_MODEL, D_FF), D_MODEL),
        "ffn2_wd": w(ks[13], (DEPTH, D_FF, D_MODEL), D_FF),
        "final_norm": gain(ks[14], (D_MODEL,)),
    }


def reference(x, ffn1_norm, ffn1_wg, ffn1_wu, ffn1_wd, mix_norm, w_in, conv_w,
              attn_sink, w_out, ffn2_norm, ffn2_wg, ffn2_wu, ffn2_wd, final_norm):
    bsz, seq, _ = x.shape
    for l in range(DEPTH):
        x = x + 0.5 * swiglu(rms_norm(x, ffn1_norm[l]), ffn1_wg[l], ffn1_wu[l], ffn1_wd[l])

        h = rms_norm(x, mix_norm[l])
        z = h @ w_in[l]
        q = z[..., :Q_END].reshape(bsz, seq, N_Q_HEADS, HEAD_DIM)
        k = z[..., Q_END:K_END].reshape(bsz, seq, N_KV_HEADS, HEAD_DIM)
        v = z[..., K_END:V_END].reshape(bsz, seq, N_KV_HEADS, HEAD_DIM)
        attn_out = sliding_window_attention(q, k, v, attn_sink[l])
        conv_out = short_conv(z[..., V_END:CB_END], z[..., CB_END:CC_END],
                              z[..., CC_END:CH_END], conv_w[l])
        x = x + jnp.concatenate([attn_out, conv_out], axis=-1) @ w_out[l]

        x = x + 0.5 * swiglu(rms_norm(x, ffn2_norm[l]), ffn2_wg[l], ffn2_wu[l], ffn2_wd[l])
    return rms_norm(x, final_norm)
```

```python
import functools
import math

import numpy as np
import jax
import jax.numpy as jnp
from jax import lax
from jax.experimental import pallas as pl
from jax.experimental.pallas import tpu as pltpu

D_MODEL = 1024
HEAD_DIM = 64
N_Q_HEADS = 8
N_KV_HEADS = 2
GQA_GROUP = N_Q_HEADS // N_KV_HEADS
ATTN_WIDTH = N_Q_HEADS * HEAD_DIM
KV_WIDTH = N_KV_HEADS * HEAD_DIM
CONV_WIDTH = 512
MIX_WIDTH = ATTN_WIDTH + CONV_WIDTH
CONV_K = 3
WINDOW = 128
BLOCK = 128
D_FF = 2816
NORM_EPS = 1e-6

Q_END = ATTN_WIDTH
KV_END = Q_END + 2 * KV_WIDTH
CB_END = KV_END + CONV_WIDTH
CC_END = CB_END + CONV_WIDTH
CH_END = CC_END + CONV_WIDTH

V7X_LANES = 128
V7X_SUBLANES = 8
V7X_MXU_DIM = 256
V7X_VMEM_BYTES = 64 * 1024 * 1024

TOKEN_TILE = 512
FF_CHUNK = V7X_MXU_DIM
MASK_VALUE = -1e30

_F32 = jnp.float32
_BF16 = jnp.bfloat16


def _vmem_limit(resident_bytes, tile_bytes, temp_bytes):
    need = resident_bytes + 2 * tile_bytes + temp_bytes
    assert need <= V7X_VMEM_BYTES, need
    return int(need)


def _rms_norm(x, gain):
    inv = lax.rsqrt(jnp.mean(x * x, axis=-1, keepdims=True) + NORM_EPS)
    return (x * inv) * gain


def _resident(shape):
    zeros = (0,) * len(shape)
    return pl.BlockSpec(shape, lambda i: zeros, pipeline_mode=pl.Buffered(1))


def _ffn_body(x_ref, gain_ref, wg_ref, wu_ref, wd_ref, fgain_ref, o_ref, *, final_norm):
    x = x_ref[...]
    h = _rms_norm(x, gain_ref[...]).astype(_BF16)
    acc = None
    for c0 in range(0, D_FF, FF_CHUNK):
        g = jnp.dot(h, wg_ref[:, c0:c0 + FF_CHUNK], preferred_element_type=_F32)
        u = jnp.dot(h, wu_ref[:, c0:c0 + FF_CHUNK], preferred_element_type=_F32)
        a = (g / (1.0 + jnp.exp(-g)) * u).astype(_BF16)
        part = jnp.dot(a, wd_ref[c0:c0 + FF_CHUNK, :], preferred_element_type=_F32)
        acc = part if acc is None else acc + part
    y = x + 0.5 * acc
    if final_norm:
        y = _rms_norm(y, fgain_ref[...])
    o_ref[...] = y


def _ffn(x, gain, wg, wu, wd, fgain, *, final_norm):
    tokens = x.shape[0]
    tm = TOKEN_TILE
    assert tokens % tm == 0 and D_FF % FF_CHUNK == 0
    tile_bytes = 2 * tm * D_MODEL * 4
    resident = 3 * D_MODEL * D_FF * 2 + 2 * D_MODEL * 4
    temps = tm * D_MODEL * (2 + 4 + 4) + 3 * tm * FF_CHUNK * 4 * 2
    row = pl.BlockSpec((tm, D_MODEL), lambda i: (i, 0))
    return pl.pallas_call(
        functools.partial(_ffn_body, final_norm=final_norm),
        grid=(tokens // tm,),
        in_specs=[row, _resident((1, D_MODEL)), _resident((D_MODEL, D_FF)),
                  _resident((D_MODEL, D_FF)), _resident((D_FF, D_MODEL)),
                  _resident((1, D_MODEL))],
        out_specs=row,
        out_shape=jax.ShapeDtypeStruct(x.shape, _F32),
        compiler_params=pltpu.CompilerParams(
            dimension_semantics=("parallel",),
            vmem_limit_bytes=_vmem_limit(resident, tile_bytes, temps)),
        name="ffn_final" if final_norm else "ffn",
    )(x, gain, wg, wu, wd, fgain)


def _attn_bias_table():
    qi = np.arange(BLOCK)[:, None]
    kj = np.arange(2 * BLOCK)[None, :]
    dist = (qi + BLOCK - kj).astype(np.float32)
    in_window = (dist >= 0) & (dist < WINDOW)
    slopes = np.exp2(-8.0 * np.arange(1, N_Q_HEADS + 1, dtype=np.float32) / N_Q_HEADS)
    bias = -slopes[:, None, None] * dist[None]
    return np.where(in_window[None], bias, np.float32(MASK_VALUE)).astype(np.float32)


def _place_heads(t):
    lo = lax.broadcasted_iota(jnp.int32, t.shape, 1) < HEAD_DIM
    r = pltpu.roll(t, HEAD_DIM, axis=1)
    zero = jnp.zeros_like(t)
    placed = (jnp.where(lo, t, zero), jnp.where(lo, r, zero),
              jnp.where(lo, zero, r), jnp.where(lo, zero, t))
    return [p.astype(_BF16) for p in placed]


def _mixer_body(sink_ref, x_ref, gain_ref, win_ref, convw_ref, wout_ref, bias_ref, o_ref,
                kp_ref, vp_ref, u_ref, mix_ref, *, tiles_per_seq):
    tm = x_ref.shape[0]
    first = pl.program_id(0) % tiles_per_seq == 0

    @pl.when(first)
    def _():
        kp_ref[:, 0:BLOCK, :] = jnp.zeros((4, BLOCK, V7X_LANES), _BF16)
        vp_ref[:, 0:BLOCK, :] = jnp.zeros((4, BLOCK, V7X_LANES), _BF16)
        u_ref[0:V7X_SUBLANES, :] = jnp.zeros((V7X_SUBLANES, CONV_WIDTH), _F32)

    x = x_ref[...]
    h = _rms_norm(x, gain_ref[...]).astype(_BF16)

    def proj(c0, c1):
        return jnp.dot(h, win_ref[:, c0:c1], preferred_element_type=_F32)

    q = (proj(0, Q_END) * (1.0 / math.sqrt(HEAD_DIM))).astype(_BF16)
    kv = proj(Q_END, KV_END)
    for idx, p in enumerate(_place_heads(kv[:, :KV_WIDTH])):
        kp_ref[idx, BLOCK:BLOCK + tm, :] = p
    for idx, p in enumerate(_place_heads(kv[:, KV_WIDTH:])):
        vp_ref[idx, BLOCK:BLOCK + tm, :] = p

    cb = proj(KV_END, CB_END)
    u = proj(CB_END, CC_END) * proj(CC_END, CH_END)
    u_ref[V7X_SUBLANES:V7X_SUBLANES + tm, :] = u
    cw = convw_ref[...]
    y = (cw[0:1, :] * u_ref[V7X_SUBLANES - 2:V7X_SUBLANES - 2 + tm, :]
         + cw[1:2, :] * u_ref[V7X_SUBLANES - 1:V7X_SUBLANES - 1 + tm, :]
         + cw[2:3, :] * u)
    mix_ref[:, ATTN_WIDTH:] = (cb * y).astype(_BF16)
    u_ref[0:V7X_SUBLANES, :] = u_ref[tm:tm + V7X_SUBLANES, :]

    prev_mask = jnp.where(first, MASK_VALUE, 0.0).astype(_F32)
    col = lax.broadcasted_iota(jnp.int32, (1, 2 * BLOCK), 1)
    first_block_bias = jnp.where(col < BLOCK, prev_mask, 0.0)
    nt = (((1,), (1,)), ((), ()))
    for j in range(tm // BLOCK):
        band = slice(j * BLOCK, (j + 2) * BLOCK)
        rows = slice(j * BLOCK, (j + 1) * BLOCK)
        for pair in range(N_Q_HEADS // 2):
            g = pair // (GQA_GROUP // 2)
            lanes = slice(pair * V7X_LANES, (pair + 1) * V7X_LANES)
            qp = q[rows, lanes]
            out = None
            for half in range(2):
                head = 2 * pair + half
                s = lax.dot_general(qp, kp_ref[2 * half + g, band, :], nt,
                                    preferred_element_type=_F32)
                s = s + bias_ref[head]
                if j == 0:
                    s = s + first_block_bias
                sink = sink_ref[head]
                m = jnp.maximum(jnp.max(s, axis=-1, keepdims=True), sink)
                p = jnp.exp(s - m)
                denom = jnp.sum(p, axis=-1, keepdims=True) + jnp.exp(sink - m)
                pn = (p / denom).astype(_BF16)
                o = jnp.dot(pn, vp_ref[2 * half + g, band, :], preferred_element_type=_F32)
                out = o if out is None else out + o
            mix_ref[rows, lanes] = out.astype(_BF16)

    kp_ref[:, 0:BLOCK, :] = kp_ref[:, tm:tm + BLOCK, :]
    vp_ref[:, 0:BLOCK, :] = vp_ref[:, tm:tm + BLOCK, :]

    o_ref[...] = x + jnp.dot(mix_ref[...], wout_ref[...], preferred_element_type=_F32)


def _mixer(x, gain, w_in, conv_w, sink, w_out, seq_len):
    tokens = x.shape[0]
    tm = TOKEN_TILE
    assert seq_len % tm == 0 and tm % BLOCK == 0
    bias = jnp.asarray(_attn_bias_table())
    tile_bytes = 2 * tm * D_MODEL * 4
    resident = (D_MODEL * CH_END * 2 + MIX_WIDTH * D_MODEL * 2 + bias.size * 4
                + (D_MODEL + CONV_K * CONV_WIDTH) * 4)
    scratch = (2 * 4 * (BLOCK + tm) * V7X_LANES * 2 + (V7X_SUBLANES + tm) * CONV_WIDTH * 4
               + tm * MIX_WIDTH * 2)
    temps = tm * D_MODEL * (2 + 4) + tm * CH_END * 4 + 8 * 1024 * 1024
    row = pl.BlockSpec((tm, D_MODEL), lambda i: (i, 0))
    return pl.pallas_call(
        functools.partial(_mixer_body, tiles_per_seq=seq_len // tm),
        grid=(tokens // tm,),
        in_specs=[pl.BlockSpec(memory_space=pltpu.SMEM), row, _resident((1, D_MODEL)),
                  _resident((D_MODEL, CH_END)), _resident((CONV_K, CONV_WIDTH)),
                  _resident((MIX_WIDTH, D_MODEL)), _resident(bias.shape)],
        out_specs=row,
        out_shape=jax.ShapeDtypeStruct(x.shape, _F32),
        scratch_shapes=[pltpu.VMEM((4, BLOCK + tm, V7X_LANES), _BF16),
                        pltpu.VMEM((4, BLOCK + tm, V7X_LANES), _BF16),
                        pltpu.VMEM((V7X_SUBLANES + tm, CONV_WIDTH), _F32),
                        pltpu.VMEM((tm, MIX_WIDTH), _BF16)],
        compiler_params=pltpu.CompilerParams(
            dimension_semantics=("arbitrary",),
            vmem_limit_bytes=_vmem_limit(resident + scratch, tile_bytes, temps)),
        name="mixer",
    )(sink, x, gain, w_in, conv_w, w_out, bias)


def kernel(x, ffn1_norm, ffn1_wg, ffn1_wu, ffn1_wd, mix_norm, w_in, conv_w, attn_sink, w_out,
           ffn2_norm, ffn2_wg, ffn2_wu, ffn2_wd, final_norm):
    bsz, seq, d = x.shape
    depth = w_in.shape[0]
    bf = lambda w: w.astype(_BF16)
    xt = x.reshape(bsz * seq, d)
    fgain = final_norm.reshape(1, d)
    for l in range(depth):
        xt = _ffn(xt, ffn1_norm[l].reshape(1, d), bf(ffn1_wg[l]), bf(ffn1_wu[l]), bf(ffn1_wd[l]),
                  fgain, final_norm=False)
        xt = _mixer(xt, mix_norm[l].reshape(1, d), bf(w_in[l]), conv_w[l], attn_sink[l],
                    bf(w_out[l]), seq)
        xt = _ffn(xt, ffn2_norm[l].reshape(1, d), bf(ffn2_wg[l]), bf(ffn2_wu[l]), bf(ffn2_wd[l]),
                  fgain, final_norm=(l == depth - 1))
    return xt.reshape(bsz, seq, d)
```

```python
import functools
import math

import numpy as np
import jax
import jax.numpy as jnp
from jax import lax
from jax.experimental import pallas as pl
from jax.experimental.pallas import tpu as pltpu

D_MODEL = 1024
HEAD_DIM = 64
N_Q_HEADS = 8
N_KV_HEADS = 2
GQA_GROUP = N_Q_HEADS // N_KV_HEADS
ATTN_WIDTH = N_Q_HEADS * HEAD_DIM
KV_WIDTH = N_KV_HEADS * HEAD_DIM
CONV_WIDTH = 512
MIX_WIDTH = ATTN_WIDTH + CONV_WIDTH
CONV_K = 3
WINDOW = 128
BLOCK = 128
D_FF = 2816
NORM_EPS = 1e-6

Q_END = ATTN_WIDTH
KV_END = Q_END + 2 * KV_WIDTH
CB_END = KV_END + CONV_WIDTH
CC_END = CB_END + CONV_WIDTH
CH_END = CC_END + CONV_WIDTH

V7X_LANES = 128
V7X_SUBLANES = 8
V7X_MXU_DIM = 256
V7X_VMEM_BYTES = 64 * 1024 * 1024

TOKEN_TILE = 512
FF_CHUNK = V7X_MXU_DIM
MASK_VALUE = -1e30

_F32 = jnp.float32
_BF16 = jnp.bfloat16


def _vmem_limit(resident_bytes, tile_bytes, temp_bytes):
    need = resident_bytes + 2 * tile_bytes + temp_bytes
    assert need <= V7X_VMEM_BYTES, need
    return int(need)


def _rms_norm(x, gain):
    inv = lax.rsqrt(jnp.mean(x * x, axis=-1, keepdims=True) + NORM_EPS)
    return (x * inv) * gain


def _resident(shape):
    zeros = (0,) * len(shape)
    return pl.BlockSpec(shape, lambda i: zeros, pipeline_mode=pl.Buffered(1))


def _ffn_body(x_ref, gain_ref, wg_ref, wu_ref, wd_ref, fgain_ref, o_ref, *, final_norm):
    x = x_ref[...]
    h = _rms_norm(x, gain_ref[...]).astype(_BF16)
    acc = None
    for c0 in range(0, D_FF, FF_CHUNK):
        g = jnp.dot(h, wg_ref[:, c0:c0 + FF_CHUNK], preferred_element_type=_F32)
        u = jnp.dot(h, wu_ref[:, c0:c0 + FF_CHUNK], preferred_element_type=_F32)
        a = (g / (1.0 + jnp.exp(-g)) * u).astype(_BF16)
        part = jnp.dot(a, wd_ref[c0:c0 + FF_CHUNK, :], preferred_element_type=_F32)
        acc = part if acc is None else acc + part
    y = x + 0.5 * acc
    if final_norm:
        y = _rms_norm(y, fgain_ref[...])
    o_ref[...] = y


def _ffn(x, gain, wg, wu, wd, fgain, *, final_norm):
    tokens = x.shape[0]
    tm = TOKEN_TILE
    assert tokens % tm == 0 and D_FF % FF_CHUNK == 0
    tile_bytes = 2 * tm * D_MODEL * 4
    resident = 3 * D_MODEL * D_FF * 2 + 2 * D_MODEL * 4
    temps = tm * D_MODEL * (2 + 4 + 4) + 3 * tm * FF_CHUNK * 4 * 2
    row = pl.BlockSpec((tm, D_MODEL), lambda i: (i, 0))
    return pl.pallas_call(
        functools.partial(_ffn_body, final_norm=final_norm),
        grid=(tokens // tm,),
        in_specs=[row, _resident((1, D_MODEL)), _resident((D_MODEL, D_FF)),
                  _resident((D_MODEL, D_FF)), _resident((D_FF, D_MODEL)),
                  _resident((1, D_MODEL))],
        out_specs=row,
        out_shape=jax.ShapeDtypeStruct(x.shape, _F32),
        compiler_params=pltpu.CompilerParams(
            dimension_semantics=("parallel",),
            vmem_limit_bytes=_vmem_limit(resident, tile_bytes, temps)),
        name="ffn_final" if final_norm else "ffn",
    )(x, gain, wg, wu, wd, fgain)


def _attn_bias_table():
    kj = np.arange(2 * BLOCK)[:, None]
    qi = np.arange(BLOCK)[None, :]
    dist = (qi + BLOCK - kj).astype(np.float32)
    in_window = (dist >= 0) & (dist < WINDOW)
    slopes = np.exp2(-8.0 * np.arange(1, N_Q_HEADS + 1, dtype=np.float32) / N_Q_HEADS)
    per_head = np.where(in_window[None], -slopes[:, None, None] * dist[None], np.float32(MASK_VALUE))
    table = [np.concatenate([per_head[GQA_GROUP * g + half], per_head[GQA_GROUP * g + 2 + half]], axis=1)
             for g in range(N_KV_HEADS) for half in range(2)]
    return np.stack(table).astype(np.float32)


def _place_heads(t):
    lo = lax.broadcasted_iota(jnp.int32, t.shape, 1) < HEAD_DIM
    r = pltpu.roll(t, HEAD_DIM, axis=1)
    zero = jnp.zeros_like(t)
    placed = (jnp.where(lo, t, zero), jnp.where(lo, r, zero),
              jnp.where(lo, zero, r), jnp.where(lo, zero, t))
    return [p.astype(_BF16) for p in placed]


def _mixer_body(sink_ref, x_ref, gain_ref, win_ref, convw_ref, wout_ref, bias_ref, o_ref,
                kp_ref, vt_ref, u_ref, mix_ref, s_ref, *, tiles_per_seq):
    tm = x_ref.shape[0]
    first = pl.program_id(0) % tiles_per_seq == 0

    @pl.when(first)
    def _():
        kp_ref[:, 0:BLOCK, :] = jnp.zeros((4, BLOCK, V7X_LANES), _BF16)
        vt_ref[:, 0:BLOCK] = jnp.zeros((KV_WIDTH, BLOCK), _BF16)
        u_ref[0:V7X_SUBLANES, :] = jnp.zeros((V7X_SUBLANES, CONV_WIDTH), _F32)

    x = x_ref[...]
    h = _rms_norm(x, gain_ref[...]).astype(_BF16)

    def proj(c0, c1):
        return jnp.dot(h, win_ref[:, c0:c1], preferred_element_type=_F32)

    q = (proj(0, Q_END) * (1.0 / math.sqrt(HEAD_DIM))).astype(_BF16)
    kv = proj(Q_END, KV_END)
    for idx, p in enumerate(_place_heads(kv[:, :KV_WIDTH])):
        kp_ref[idx, BLOCK:BLOCK + tm, :] = p
    vt_ref[:, BLOCK:BLOCK + tm] = kv[:, KV_WIDTH:].T.astype(_BF16)

    conv = {}

    def conv_gate():
        conv["cb"] = proj(KV_END, CB_END)

    def conv_in():
        conv["cc"] = proj(CB_END, CC_END)

    def conv_out():
        u = conv["cc"] * proj(CC_END, CH_END)
        u_ref[V7X_SUBLANES:V7X_SUBLANES + tm, :] = u
        cw = convw_ref[...]
        y = (cw[0:1, :] * u_ref[V7X_SUBLANES - 2:V7X_SUBLANES - 2 + tm, :]
             + cw[1:2, :] * u_ref[V7X_SUBLANES - 1:V7X_SUBLANES - 1 + tm, :]
             + cw[2:3, :] * u)
        mix_ref[:, ATTN_WIDTH:] = (conv["cb"] * y).astype(_BF16)
        u_ref[0:V7X_SUBLANES, :] = u_ref[tm:tm + V7X_SUBLANES, :]

    def out_proj(r0, r1):
        o_ref[r0:r1, :] = x_ref[r0:r1, :] + jnp.dot(mix_ref[r0:r1, :], wout_ref[...],
                                                    preferred_element_type=_F32)

    prev_mask = jnp.where(first, MASK_VALUE, 0.0).astype(_F32)
    lane = lax.broadcasted_iota(jnp.int32, (1, 2 * BLOCK), 1)
    nt = (((1,), (1,)), ((), ()))
    chains = [(j, g, half) for j in range(tm // BLOCK) for g in range(N_KV_HEADS)
              for half in range(2)]
    half_tile = tm // 2
    chains_per_half_tile = len(chains) // 2
    fillers_a = {3: conv_gate, 9: conv_in}
    fillers_b = {1: conv_out, chains_per_half_tile + 1: functools.partial(out_proj, 0, half_tile)}

    def scores(j, g, half):
        rows = slice(j * BLOCK, (j + 1) * BLOCK)
        qg = jnp.concatenate([q[rows, (2 * g) * V7X_LANES:(2 * g + 1) * V7X_LANES],
                              q[rows, (2 * g + 1) * V7X_LANES:(2 * g + 2) * V7X_LANES]], axis=0)
        return lax.dot_general(kp_ref[2 * half + g, j * BLOCK:(j + 2) * BLOCK, :], qg, nt,
                               preferred_element_type=_F32)

    stats = []
    for c, (j, g, half) in enumerate(chains):
        s = scores(j, g, half) + bias_ref[2 * g + half]
        if j == 0:
            s = jnp.concatenate([s[:BLOCK] + prev_mask, s[BLOCK:]], axis=0)
        s_ref[c] = s
        sink = jnp.where(lane < BLOCK, sink_ref[GQA_GROUP * g + half],
                         sink_ref[GQA_GROUP * g + 2 + half])
        stats.append((jnp.maximum(jnp.max(s, axis=0, keepdims=True), sink), sink))
        if c in fillers_a:
            fillers_a[c]()

    outs = {}
    for c, (j, g, half) in enumerate(chains):
        if c in fillers_b:
            fillers_b[c]()
        m, sink = stats[c]
        p = jnp.exp(s_ref[c] - m)
        denom = jnp.sum(p, axis=0, keepdims=True) + jnp.exp(sink - m)
        o = jnp.dot(vt_ref[:, j * BLOCK:(j + 2) * BLOCK], p.astype(_BF16),
                    preferred_element_type=_F32)
        outs[half] = o[HEAD_DIM * g:HEAD_DIM * (g + 1), :] * (1.0 / denom)
        if half == 1:
            for pr in range(2):
                qcols = slice(pr * BLOCK, (pr + 1) * BLOCK)
                at = jnp.concatenate([outs[0][:, qcols], outs[1][:, qcols]], axis=0)
                mix_ref[j * BLOCK:(j + 1) * BLOCK,
                        (2 * g + pr) * V7X_LANES:(2 * g + pr + 1) * V7X_LANES] = at.T.astype(_BF16)
    out_proj(half_tile, tm)

    kp_ref[:, 0:BLOCK, :] = kp_ref[:, tm:tm + BLOCK, :]
    vt_ref[:, 0:BLOCK] = vt_ref[:, tm:tm + BLOCK]


def _mixer(x, gain, w_in, conv_w, sink, w_out, seq_len):
    tokens = x.shape[0]
    tm = TOKEN_TILE
    assert seq_len % tm == 0 and tm % BLOCK == 0
    bias = jnp.asarray(_attn_bias_table())
    n_chains = (tm // BLOCK) * N_KV_HEADS * 2
    tile_bytes = 2 * tm * D_MODEL * 4
    resident = (D_MODEL * CH_END * 2 + MIX_WIDTH * D_MODEL * 2 + bias.size * 4
                + (D_MODEL + CONV_K * CONV_WIDTH) * 4)
    scratch = ((4 + 1) * (BLOCK + tm) * V7X_LANES * 2 + (V7X_SUBLANES + tm) * CONV_WIDTH * 4
               + tm * MIX_WIDTH * 2 + n_chains * 4 * BLOCK * BLOCK * 4)
    temps = tm * D_MODEL * (2 + 4) + tm * CH_END * 4 + 8 * 1024 * 1024
    row = pl.BlockSpec((tm, D_MODEL), lambda i: (i, 0))
    return pl.pallas_call(
        functools.partial(_mixer_body, tiles_per_seq=seq_len // tm),
        grid=(tokens // tm,),
        in_specs=[pl.BlockSpec(memory_space=pltpu.SMEM), row, _resident((1, D_MODEL)),
                  _resident((D_MODEL, CH_END)), _resident((CONV_K, CONV_WIDTH)),
                  _resident((MIX_WIDTH, D_MODEL)), _resident(bias.shape)],
        out_specs=row,
        out_shape=jax.ShapeDtypeStruct(x.shape, _F32),
        scratch_shapes=[pltpu.VMEM((4, BLOCK + tm, V7X_LANES), _BF16),
                        pltpu.VMEM((KV_WIDTH, BLOCK + tm), _BF16),
                        pltpu.VMEM((V7X_SUBLANES + tm, CONV_WIDTH), _F32),
                        pltpu.VMEM((tm, MIX_WIDTH), _BF16),
                        pltpu.VMEM((n_chains, 2 * BLOCK, 2 * BLOCK), _F32)],
        compiler_params=pltpu.CompilerParams(
            dimension_semantics=("arbitrary",),
            vmem_limit_bytes=_vmem_limit(resident + scratch, tile_bytes, temps)),
        name="mixer",
    )(sink, x, gain, w_in, conv_w, w_out, bias)


def kernel(x, ffn1_norm, ffn1_wg, ffn1_wu, ffn1_wd, mix_norm, w_in, conv_w, attn_sink, w_out,
           ffn2_norm, ffn2_wg, ffn2_wu, ffn2_wd, final_norm):
    bsz, seq, d = x.shape
    depth = w_in.shape[0]
    bf = lambda w: w.astype(_BF16)
    xt = x.reshape(bsz * seq, d)
    fgain = final_norm.reshape(1, d)
    for l in range(depth):
        xt = _ffn(xt, ffn1_norm[l].reshape(1, d), bf(ffn1_wg[l]), bf(ffn1_wu[l]), bf(ffn1_wd[l]),
                  fgain, final_norm=False)
        xt = _mixer(xt, mix_norm[l].reshape(1, d), bf(w_in[l]), conv_w[l], attn_sink[l],
                    bf(w_out[l]), seq)
        xt = _ffn(xt, ffn2_norm[l].reshape(1, d), bf(ffn2_wg[l]), bf(ffn2_wu[l]), bf(ffn2_wd[l]),
                  fgain, final_norm=(l == depth - 1))
    return xt.reshape(bsz, seq, d)
```

```python
import functools
import math

import numpy as np
import jax
import jax.numpy as jnp
from jax import lax
from jax.experimental import pallas as pl
from jax.experimental.pallas import tpu as pltpu

D_MODEL = 1024
HEAD_DIM = 64
N_Q_HEADS = 8
N_KV_HEADS = 2
GQA_GROUP = N_Q_HEADS // N_KV_HEADS
ATTN_WIDTH = N_Q_HEADS * HEAD_DIM
KV_WIDTH = N_KV_HEADS * HEAD_DIM
CONV_WIDTH = 512
MIX_WIDTH = ATTN_WIDTH + CONV_WIDTH
CONV_K = 3
WINDOW = 128
BLOCK = 128
D_FF = 2816
NORM_EPS = 1e-6

Q_END = ATTN_WIDTH
KV_END = Q_END + 2 * KV_WIDTH
CB_END = KV_END + CONV_WIDTH
CC_END = CB_END + CONV_WIDTH
CH_END = CC_END + CONV_WIDTH

V7X_LANES = 128
V7X_SUBLANES = 8
V7X_MXU_DIM = 256
V7X_VMEM_BYTES = 64 * 1024 * 1024

TOKEN_TILE = 512
FFN_TOKEN_TILE = 1024
FFN_SUB_TILE = 512
FFN_STAGGER = 6
FF_CHUNK = V7X_MXU_DIM
WEIGHT_STAGE_BYTES = 768 * 1024
MASK_VALUE = -1e30
LOG2E = math.log2(math.e)
PRE_FILLERS = ("conv_in", "conv_taps", "q", "kv", "conv_gate")
FILLERS_A = {}
FILLERS_B = {9: "out_first"}

_F32 = jnp.float32
_BF16 = jnp.bfloat16


def _vmem_limit(resident_bytes, tile_bytes, temp_bytes):
    need = resident_bytes + 2 * tile_bytes + temp_bytes
    assert need <= V7X_VMEM_BYTES, need
    return int(need)


def _rms_norm(x, gain):
    inv = lax.rsqrt(jnp.mean(x * x, axis=-1, keepdims=True) + NORM_EPS)
    return (x * inv) * gain


def _resident(shape):
    zeros = (0,) * len(shape)
    return pl.BlockSpec(shape, lambda i: zeros, pipeline_mode=pl.Buffered(1))


_HBM = pl.BlockSpec(memory_space=pl.ANY)


def _stage_shape(weight_shape):
    rows, cols = weight_shape
    bf16_rows = 2 * V7X_SUBLANES
    fits = [r for r in range(bf16_rows, WEIGHT_STAGE_BYTES // (cols * 4) + 1, bf16_rows)
            if rows % r == 0]
    return (2, max(fits), cols)


def _load_weight_bf16(src_hbm, dst_ref, stage_ref, sem_ref):
    chunk_rows = stage_ref.shape[1]
    n_chunks = dst_ref.shape[0] // chunk_rows

    def copy(k):
        return pltpu.make_async_copy(src_hbm.at[pl.ds(k * chunk_rows, chunk_rows), :],
                                     stage_ref.at[k % 2], sem_ref.at[k % 2])

    copy(0).start()
    for k in range(n_chunks):
        if k + 1 < n_chunks:
            copy(k + 1).start()
        copy(k).wait()
        dst_ref[k * chunk_rows:(k + 1) * chunk_rows, :] = stage_ref[k % 2].astype(_BF16)


def _ffn_body(x_ref, gain_ref, wg_hbm, wu_hbm, wd_hbm, fgain_ref, o_ref,
              h_ref, wg_ref, wu_ref, wd_ref, stage_in_ref, stage_out_ref, sem_ref,
              *, layer, final_norm):
    @pl.when(pl.program_id(0) == 0)
    def _():
        _load_weight_bf16(wg_hbm.at[layer], wg_ref, stage_in_ref, sem_ref)
        _load_weight_bf16(wu_hbm.at[layer], wu_ref, stage_in_ref, sem_ref)
        _load_weight_bf16(wd_hbm.at[layer], wd_ref, stage_out_ref, sem_ref)

    subs = [slice(r0, r0 + FFN_SUB_TILE) for r0 in range(0, x_ref.shape[0], FFN_SUB_TILE)]
    chunks = list(range(0, D_FF, FF_CHUNK))
    order = sorted((ci + si * FFN_STAGGER, si, ci)
                   for si in range(len(subs)) for ci in range(len(chunks)))
    acc = [None] * len(subs)
    for _, si, ci in order:
        rows, c0 = subs[si], chunks[ci]
        if ci == 0:
            h_ref[rows, :] = _rms_norm(x_ref[rows, :], gain_ref[...]).astype(_BF16)
        h = h_ref[rows, :]
        g = jnp.dot(h, wg_ref[:, c0:c0 + FF_CHUNK], preferred_element_type=_F32)
        u = jnp.dot(h, wu_ref[:, c0:c0 + FF_CHUNK], preferred_element_type=_F32)
        a = (g / (1.0 + jnp.exp(-g)) * u).astype(_BF16)
        part = jnp.dot(a, wd_ref[c0:c0 + FF_CHUNK, :], preferred_element_type=_F32)
        acc[si] = part if acc[si] is None else acc[si] + part
        if ci == len(chunks) - 1:
            y = x_ref[rows, :] + 0.5 * acc[si]
            if final_norm:
                y = _rms_norm(y, fgain_ref[...])
            o_ref[rows, :] = y


def _ffn(x, gain, wg, wu, wd, fgain, *, layer, final_norm):
    tokens = x.shape[0]
    tm = FFN_TOKEN_TILE
    assert tokens % tm == 0 and tm % FFN_SUB_TILE == 0 and D_FF % FF_CHUNK == 0
    stage_in, stage_out = _stage_shape((D_MODEL, D_FF)), _stage_shape((D_FF, D_MODEL))
    tile_bytes = 2 * tm * D_MODEL * 4
    resident = (3 * D_MODEL * D_FF * 2 + 2 * D_MODEL * 4
                + 4 * (math.prod(stage_in) + math.prod(stage_out)))
    temps = tm * D_MODEL * (2 + 4 + 4) + 3 * tm * FF_CHUNK * 4 * 2
    row = pl.BlockSpec((tm, D_MODEL), lambda i: (i, 0))
    return pl.pallas_call(
        functools.partial(_ffn_body, layer=layer, final_norm=final_norm),
        grid=(tokens // tm,),
        in_specs=[row, _resident((1, D_MODEL)), _HBM, _HBM, _HBM, _resident((1, D_MODEL))],
        out_specs=row,
        out_shape=jax.ShapeDtypeStruct(x.shape, _F32),
        scratch_shapes=[pltpu.VMEM((tm, D_MODEL), _BF16),
                        pltpu.VMEM((D_MODEL, D_FF), _BF16), pltpu.VMEM((D_MODEL, D_FF), _BF16),
                        pltpu.VMEM((D_FF, D_MODEL), _BF16),
                        pltpu.VMEM(stage_in, _F32), pltpu.VMEM(stage_out, _F32),
                        pltpu.SemaphoreType.DMA((2,))],
        compiler_params=pltpu.CompilerParams(
            dimension_semantics=("arbitrary",),
            vmem_limit_bytes=_vmem_limit(resident, tile_bytes, temps)),
        name="ffn_final" if final_norm else "ffn",
    )(x, gain, wg, wu, wd, fgain)


def _attn_bias_table():
    kj = np.arange(2 * BLOCK)[:, None]
    qi = np.arange(BLOCK)[None, :]
    dist = (qi + BLOCK - kj).astype(np.float32)
    in_window = (dist >= 0) & (dist < WINDOW)
    slopes = np.exp2(-8.0 * np.arange(1, N_Q_HEADS + 1, dtype=np.float32) / N_Q_HEADS)
    per_head = np.where(in_window[None], -slopes[:, None, None] * dist[None] * np.float32(LOG2E),
                        np.float32(MASK_VALUE))
    table = [np.concatenate([per_head[GQA_GROUP * g + half], per_head[GQA_GROUP * g + 2 + half]], axis=1)
             for g in range(N_KV_HEADS) for half in range(2)]
    return np.stack(table).astype(np.float32)


def _place_heads(t):
    lo = lax.broadcasted_iota(jnp.int32, t.shape, 1) < HEAD_DIM
    r = pltpu.roll(t, HEAD_DIM, axis=1)
    zero = jnp.zeros_like(t)
    placed = (jnp.where(lo, t, zero), jnp.where(lo, r, zero),
              jnp.where(lo, zero, r), jnp.where(lo, zero, t))
    return [p.astype(_BF16) for p in placed]


def _mixer_body(sink_ref, x_ref, gain_ref, win_hbm, convw_ref, wout_hbm, bias_ref, o_ref,
                kp_ref, vt_ref, u_ref, mix_ref, s_ref,
                win_ref, wout_ref, stage_in_ref, stage_out_ref, sem_ref, *, layer, tiles_per_seq):
    tm = x_ref.shape[0]
    first = pl.program_id(0) % tiles_per_seq == 0

    @pl.when(pl.program_id(0) == 0)
    def _():
        _load_weight_bf16(win_hbm.at[layer], win_ref, stage_in_ref, sem_ref)
        _load_weight_bf16(wout_hbm.at[layer], wout_ref, stage_out_ref, sem_ref)

    @pl.when(first)
    def _():
        kp_ref[:, 0:BLOCK, :] = jnp.zeros((4, BLOCK, V7X_LANES), _BF16)
        vt_ref[:, 0:BLOCK] = jnp.zeros((KV_WIDTH, BLOCK), _BF16)
        u_ref[0:V7X_SUBLANES, :] = jnp.zeros((V7X_SUBLANES, CONV_WIDTH), _F32)

    x = x_ref[...]
    h = _rms_norm(x, gain_ref[...]).astype(_BF16)

    def proj(c0, c1):
        return jnp.dot(h, win_ref[:, c0:c1], preferred_element_type=_F32)

    val = {}

    def step_q():
        val["q"] = (proj(0, Q_END) * (LOG2E / math.sqrt(HEAD_DIM))).astype(_BF16)

    def step_kv():
        kv = proj(Q_END, KV_END)
        for idx, p in enumerate(_place_heads(kv[:, :KV_WIDTH])):
            kp_ref[idx, BLOCK:BLOCK + tm, :] = p
        vt_ref[:, BLOCK:BLOCK + tm] = kv[:, KV_WIDTH:].T.astype(_BF16)

    def conv_in():
        val["cc"] = proj(CB_END, CC_END)

    def conv_taps():
        u = val["cc"] * proj(CC_END, CH_END)
        u_ref[V7X_SUBLANES:V7X_SUBLANES + tm, :] = u
        cw = convw_ref[...]
        val["y"] = (cw[0:1, :] * u_ref[V7X_SUBLANES - 2:V7X_SUBLANES - 2 + tm, :]
                    + cw[1:2, :] * u_ref[V7X_SUBLANES - 1:V7X_SUBLANES - 1 + tm, :]
                    + cw[2:3, :] * u)
        u_ref[0:V7X_SUBLANES, :] = u_ref[tm:tm + V7X_SUBLANES, :]

    def conv_gate():
        mix_ref[:, ATTN_WIDTH:] = (proj(KV_END, CB_END) * val["y"]).astype(_BF16)

    def out_proj(r0, r1):
        o_ref[r0:r1, :] = x_ref[r0:r1, :] + jnp.dot(mix_ref[r0:r1, :], wout_ref[...],
                                                    preferred_element_type=_F32)

    prev_mask = jnp.where(first, MASK_VALUE, 0.0).astype(_F32)
    lane = lax.broadcasted_iota(jnp.int32, (1, 2 * BLOCK), 1)
    nt = (((1,), (1,)), ((), ()))
    chains = [(j, g, half) for j in range(tm // BLOCK) for g in range(N_KV_HEADS)
              for half in range(2)]
    half_tile = tm // 2
    filler_fns = dict(q=step_q, kv=step_kv, conv_in=conv_in, conv_taps=conv_taps,
                      conv_gate=conv_gate, out_first=functools.partial(out_proj, 0, half_tile))
    for name in PRE_FILLERS:
        filler_fns[name]()
    fillers_a = {c: filler_fns[name] for c, name in FILLERS_A.items()}
    fillers_b = {c: filler_fns[name] for c, name in FILLERS_B.items()}

    def scores(j, g, half):
        rows = slice(j * BLOCK, (j + 1) * BLOCK)
        q = val["q"]
        qg = jnp.concatenate([q[rows, (2 * g) * V7X_LANES:(2 * g + 1) * V7X_LANES],
                              q[rows, (2 * g + 1) * V7X_LANES:(2 * g + 2) * V7X_LANES]], axis=0)
        return lax.dot_general(kp_ref[2 * half + g, j * BLOCK:(j + 2) * BLOCK, :], qg, nt,
                               preferred_element_type=_F32)

    stats = []
    for c, (j, g, half) in enumerate(chains):
        s = scores(j, g, half) + bias_ref[2 * g + half]
        if j == 0:
            s = jnp.concatenate([s[:BLOCK] + prev_mask, s[BLOCK:]], axis=0)
        s_ref[c] = s
        sink = LOG2E * jnp.where(lane < BLOCK, sink_ref[GQA_GROUP * g + half],
                                 sink_ref[GQA_GROUP * g + 2 + half])
        stats.append((jnp.maximum(jnp.max(s, axis=0, keepdims=True), sink), sink))
        if c in fillers_a:
            fillers_a[c]()

    outs = {}
    for c, (j, g, half) in enumerate(chains):
        if c in fillers_b:
            fillers_b[c]()
        m, sink = stats[c]
        p = jnp.exp2(s_ref[c] - m)
        denom = jnp.sum(p, axis=0, keepdims=True) + jnp.exp2(sink - m)
        o = jnp.dot(vt_ref[:, j * BLOCK:(j + 2) * BLOCK], p.astype(_BF16),
                    preferred_element_type=_F32)
        outs[half] = o[HEAD_DIM * g:HEAD_DIM * (g + 1), :] * (1.0 / denom)
        if half == 1:
            for pr in range(2):
                qcols = slice(pr * BLOCK, (pr + 1) * BLOCK)
                at = jnp.concatenate([outs[0][:, qcols], outs[1][:, qcols]], axis=0)
                mix_ref[j * BLOCK:(j + 1) * BLOCK,
                        (2 * g + pr) * V7X_LANES:(2 * g + pr + 1) * V7X_LANES] = at.T.astype(_BF16)
    out_proj(half_tile, tm)

    kp_ref[:, 0:BLOCK, :] = kp_ref[:, tm:tm + BLOCK, :]
    vt_ref[:, 0:BLOCK] = vt_ref[:, tm:tm + BLOCK]


def _mixer(x, gain, w_in, conv_w, sink, w_out, seq_len, *, layer):
    tokens = x.shape[0]
    tm = TOKEN_TILE
    assert seq_len % tm == 0 and tm % BLOCK == 0
    bias = jnp.asarray(_attn_bias_table())
    n_chains = (tm // BLOCK) * N_KV_HEADS * 2
    stage_in, stage_out = _stage_shape((D_MODEL, CH_END)), _stage_shape((MIX_WIDTH, D_MODEL))
    tile_bytes = 2 * tm * D_MODEL * 4
    resident = (D_MODEL * CH_END * 2 + MIX_WIDTH * D_MODEL * 2 + bias.size * 4
                + (D_MODEL + CONV_K * CONV_WIDTH) * 4
                + 4 * (math.prod(stage_in) + math.prod(stage_out)))
    scratch = ((4 + 1) * (BLOCK + tm) * V7X_LANES * 2 + (V7X_SUBLANES + tm) * CONV_WIDTH * 4
               + tm * MIX_WIDTH * 2 + n_chains * 4 * BLOCK * BLOCK * 4)
    temps = tm * D_MODEL * (2 + 4) + tm * CH_END * 4 + 8 * 1024 * 1024
    row = pl.BlockSpec((tm, D_MODEL), lambda i: (i, 0))
    return pl.pallas_call(
        functools.partial(_mixer_body, layer=layer, tiles_per_seq=seq_len // tm),
        grid=(tokens // tm,),
        in_specs=[pl.BlockSpec(memory_space=pltpu.SMEM), row, _resident((1, D_MODEL)),
                  _HBM, _resident((CONV_K, CONV_WIDTH)), _HBM, _resident(bias.shape)],
        out_specs=row,
        out_shape=jax.ShapeDtypeStruct(x.shape, _F32),
        scratch_shapes=[pltpu.VMEM((4, BLOCK + tm, V7X_LANES), _BF16),
                        pltpu.VMEM((KV_WIDTH, BLOCK + tm), _BF16),
                        pltpu.VMEM((V7X_SUBLANES + tm, CONV_WIDTH), _F32),
                        pltpu.VMEM((tm, MIX_WIDTH), _BF16),
                        pltpu.VMEM((n_chains, 2 * BLOCK, 2 * BLOCK), _F32),
                        pltpu.VMEM((D_MODEL, CH_END), _BF16), pltpu.VMEM((MIX_WIDTH, D_MODEL), _BF16),
                        pltpu.VMEM(stage_in, _F32), pltpu.VMEM(stage_out, _F32),
                        pltpu.SemaphoreType.DMA((2,))],
        compiler_params=pltpu.CompilerParams(
            dimension_semantics=("arbitrary",),
            vmem_limit_bytes=_vmem_limit(resident + scratch, tile_bytes, temps)),
        name="mixer",
    )(sink, x, gain, w_in, conv_w, w_out, bias)


def kernel(x, ffn1_norm, ffn1_wg, ffn1_wu, ffn1_wd, mix_norm, w_in, conv_w, attn_sink, w_out,
           ffn2_norm, ffn2_wg, ffn2_wu, ffn2_wd, final_norm):
    bsz, seq, d = x.shape
    depth = w_in.shape[0]
    xt = x.reshape(bsz * seq, d)
    fgain = final_norm.reshape(1, d)
    for l in range(depth):
        xt = _ffn(xt, ffn1_norm[l].reshape(1, d), ffn1_wg, ffn1_wu, ffn1_wd, fgain,
                  layer=l, final_norm=False)
        xt = _mixer(xt, mix_norm[l].reshape(1, d), w_in, conv_w[l], attn_sink[l], w_out, seq,
                    layer=l)
        xt = _ffn(xt, ffn2_norm[l].reshape(1, d), ffn2_wg, ffn2_wu, ffn2_wd, fgain,
                  layer=l, final_norm=(l == depth - 1))
    return xt.reshape(bsz, seq, d)
```

```python
import functools
import math

import numpy as np
import jax
import jax.numpy as jnp
from jax import lax
from jax.experimental import pallas as pl
from jax.experimental.pallas import tpu as pltpu

D_MODEL = 1024
HEAD_DIM = 64
N_Q_HEADS = 8
N_KV_HEADS = 2
GQA_GROUP = N_Q_HEADS // N_KV_HEADS
ATTN_WIDTH = N_Q_HEADS * HEAD_DIM
KV_WIDTH = N_KV_HEADS * HEAD_DIM
CONV_WIDTH = 512
MIX_WIDTH = ATTN_WIDTH + CONV_WIDTH
CONV_K = 3
WINDOW = 128
BLOCK = 128
D_FF = 2816
NORM_EPS = 1e-6

Q_END = ATTN_WIDTH
KV_END = Q_END + 2 * KV_WIDTH
CB_END = KV_END + CONV_WIDTH
CC_END = CB_END + CONV_WIDTH
CH_END = CC_END + CONV_WIDTH

V7X_LANES = 128
V7X_SUBLANES = 8
V7X_MXU_DIM = 256
V7X_VMEM_BYTES = 64 * 1024 * 1024

TOKEN_TILE = 512
FFN_TOKEN_TILE = 1024
FFN_SUB_TILE = 512
FFN_STAGGER = 6
FF_CHUNK = V7X_MXU_DIM
WEIGHT_STAGE_BYTES = 1536 * 1024
WEIGHT_STAGE_SLOTS = 3
MASK_VALUE = -1e30
LOG2E = math.log2(math.e)
PRE_FILLERS = ("conv_in", "conv_taps", "q", "kv", "conv_gate")
FILLERS_A = {}
FILLERS_B = {9: "out_first"}

_F32 = jnp.float32
_BF16 = jnp.bfloat16


def _vmem_limit(resident_bytes, tile_bytes, temp_bytes):
    need = resident_bytes + 2 * tile_bytes + temp_bytes
    assert need <= V7X_VMEM_BYTES, need
    return int(need)


def _rms_norm(x, gain):
    inv = lax.rsqrt(jnp.mean(x * x, axis=-1, keepdims=True) + NORM_EPS)
    return (x * inv) * gain


def _resident(shape):
    zeros = (0,) * len(shape)
    return pl.BlockSpec(shape, lambda i: zeros, pipeline_mode=pl.Buffered(1))


_HBM = pl.BlockSpec(memory_space=pl.ANY)


def _stage_shape(weight_shape):
    rows, cols = weight_shape
    bf16_rows = 2 * V7X_SUBLANES
    fits = [r for r in range(bf16_rows, WEIGHT_STAGE_BYTES // (cols * 4) + 1, bf16_rows)
            if rows % r == 0]
    return (WEIGHT_STAGE_SLOTS, max(fits), cols)


def _load_weights_bf16(jobs, stages, sem_ref):
    chunks, used = [], [0] * len(stages)
    for src, dst, si in jobs:
        slots, rows, _ = stages[si].shape
        for r0 in range(0, dst.shape[0], rows):
            chunks.append((src, dst, si, used[si] % slots, r0, rows))
            used[si] += 1

    def copy(chunk):
        src, _, si, slot, r0, rows = chunk
        return pltpu.make_async_copy(src.at[pl.ds(r0, rows), :], stages[si].at[slot],
                                     sem_ref.at[si, slot])

    ahead = min(stage.shape[0] for stage in stages) - 1
    for chunk in chunks[:ahead]:
        copy(chunk).start()
    for k, chunk in enumerate(chunks):
        if k + ahead < len(chunks):
            copy(chunks[k + ahead]).start()
        copy(chunk).wait()
        _, dst, si, slot, r0, rows = chunk
        dst[r0:r0 + rows, :] = stages[si][slot].astype(_BF16)


def _ffn_body(x_ref, gain_ref, wg_hbm, wu_hbm, wd_hbm, fgain_ref, o_ref,
              h_ref, wg_ref, wu_ref, wd_ref, stage_in_ref, stage_out_ref, sem_ref,
              *, layer, final_norm):
    @pl.when(pl.program_id(0) == 0)
    def _():
        _load_weights_bf16([(wg_hbm.at[layer], wg_ref, 0), (wu_hbm.at[layer], wu_ref, 0),
                            (wd_hbm.at[layer], wd_ref, 1)], [stage_in_ref, stage_out_ref], sem_ref)

    subs = [slice(r0, r0 + FFN_SUB_TILE) for r0 in range(0, x_ref.shape[0], FFN_SUB_TILE)]
    chunks = list(range(0, D_FF, FF_CHUNK))
    order = sorted((ci + si * FFN_STAGGER, si, ci)
                   for si in range(len(subs)) for ci in range(len(chunks)))
    acc = [None] * len(subs)
    for _, si, ci in order:
        rows, c0 = subs[si], chunks[ci]
        if ci == 0:
            h_ref[rows, :] = _rms_norm(x_ref[rows, :], gain_ref[...]).astype(_BF16)
        h = h_ref[rows, :]
        g = jnp.dot(h, wg_ref[:, c0:c0 + FF_CHUNK], preferred_element_type=_F32)
        u = jnp.dot(h, wu_ref[:, c0:c0 + FF_CHUNK], preferred_element_type=_F32)
        a = (g / (1.0 + jnp.exp(-g)) * u).astype(_BF16)
        part = jnp.dot(a, wd_ref[c0:c0 + FF_CHUNK, :], preferred_element_type=_F32)
        acc[si] = part if acc[si] is None else acc[si] + part
        if ci == len(chunks) - 1:
            y = x_ref[rows, :] + 0.5 * acc[si]
            if final_norm:
                y = _rms_norm(y, fgain_ref[...])
            o_ref[rows, :] = y


def _ffn(x, gain, wg, wu, wd, fgain, *, layer, final_norm):
    tokens = x.shape[0]
    tm = FFN_TOKEN_TILE
    assert tokens % tm == 0 and tm % FFN_SUB_TILE == 0 and D_FF % FF_CHUNK == 0
    stage_in, stage_out = _stage_shape((D_MODEL, D_FF)), _stage_shape((D_FF, D_MODEL))
    tile_bytes = 2 * tm * D_MODEL * 4
    resident = (3 * D_MODEL * D_FF * 2 + 2 * D_MODEL * 4
                + 4 * (math.prod(stage_in) + math.prod(stage_out)))
    temps = tm * D_MODEL * (2 + 4 + 4) + 3 * tm * FF_CHUNK * 4 * 2
    row = pl.BlockSpec((tm, D_MODEL), lambda i: (i, 0))
    return pl.pallas_call(
        functools.partial(_ffn_body, layer=layer, final_norm=final_norm),
        grid=(tokens // tm,),
        in_specs=[row, _resident((1, D_MODEL)), _HBM, _HBM, _HBM, _resident((1, D_MODEL))],
        out_specs=row,
        out_shape=jax.ShapeDtypeStruct(x.shape, _F32),
        scratch_shapes=[pltpu.VMEM((tm, D_MODEL), _BF16),
                        pltpu.VMEM((D_MODEL, D_FF), _BF16), pltpu.VMEM((D_MODEL, D_FF), _BF16),
                        pltpu.VMEM((D_FF, D_MODEL), _BF16),
                        pltpu.VMEM(stage_in, _F32), pltpu.VMEM(stage_out, _F32),
                        pltpu.SemaphoreType.DMA((2, WEIGHT_STAGE_SLOTS))],
        compiler_params=pltpu.CompilerParams(
            dimension_semantics=("arbitrary",),
            vmem_limit_bytes=_vmem_limit(resident, tile_bytes, temps)),
        name="ffn_final" if final_norm else "ffn",
    )(x, gain, wg, wu, wd, fgain)


def _attn_bias_table():
    kj = np.arange(2 * BLOCK)[:, None]
    qi = np.arange(BLOCK)[None, :]
    dist = (qi + BLOCK - kj).astype(np.float32)
    in_window = (dist >= 0) & (dist < WINDOW)
    slopes = np.exp2(-8.0 * np.arange(1, N_Q_HEADS + 1, dtype=np.float32) / N_Q_HEADS)
    per_head = np.where(in_window[None], -slopes[:, None, None] * dist[None] * np.float32(LOG2E),
                        np.float32(MASK_VALUE))
    table = [np.concatenate([per_head[GQA_GROUP * g + half], per_head[GQA_GROUP * g + 2 + half]], axis=1)
             for g in range(N_KV_HEADS) for half in range(2)]
    return np.stack(table).astype(np.float32)


def _place_heads(t):
    lo = lax.broadcasted_iota(jnp.int32, t.shape, 1) < HEAD_DIM
    r = pltpu.roll(t, HEAD_DIM, axis=1)
    zero = jnp.zeros_like(t)
    placed = (jnp.where(lo, t, zero), jnp.where(lo, r, zero),
              jnp.where(lo, zero, r), jnp.where(lo, zero, t))
    return [p.astype(_BF16) for p in placed]


def _mixer_body(sink_ref, x_ref, gain_ref, win_hbm, convw_ref, wout_hbm, bias_ref, o_ref,
                kp_ref, vt_ref, u_ref, mix_ref, s_ref,
                win_ref, wout_ref, stage_in_ref, stage_out_ref, sem_ref, *, layer, tiles_per_seq):
    tm = x_ref.shape[0]
    first = pl.program_id(0) % tiles_per_seq == 0

    @pl.when(pl.program_id(0) == 0)
    def _():
        _load_weights_bf16([(win_hbm.at[layer], win_ref, 0), (wout_hbm.at[layer], wout_ref, 1)],
                           [stage_in_ref, stage_out_ref], sem_ref)

    @pl.when(first)
    def _():
        kp_ref[:, 0:BLOCK, :] = jnp.zeros((4, BLOCK, V7X_LANES), _BF16)
        vt_ref[:, 0:BLOCK] = jnp.zeros((KV_WIDTH, BLOCK), _BF16)
        u_ref[0:V7X_SUBLANES, :] = jnp.zeros((V7X_SUBLANES, CONV_WIDTH), _F32)

    x = x_ref[...]
    h = _rms_norm(x, gain_ref[...]).astype(_BF16)

    def proj(c0, c1):
        return jnp.dot(h, win_ref[:, c0:c1], preferred_element_type=_F32)

    val = {}

    def step_q():
        val["q"] = (proj(0, Q_END) * (LOG2E / math.sqrt(HEAD_DIM))).astype(_BF16)

    def step_kv():
        kv = proj(Q_END, KV_END)
        for idx, p in enumerate(_place_heads(kv[:, :KV_WIDTH])):
            kp_ref[idx, BLOCK:BLOCK + tm, :] = p
        vt_ref[:, BLOCK:BLOCK + tm] = kv[:, KV_WIDTH:].T.astype(_BF16)

    def conv_in():
        val["cc"] = proj(CB_END, CC_END)

    def conv_taps():
        u = val["cc"] * proj(CC_END, CH_END)
        u_ref[V7X_SUBLANES:V7X_SUBLANES + tm, :] = u
        cw = convw_ref[...]
        val["y"] = (cw[0:1, :] * u_ref[V7X_SUBLANES - 2:V7X_SUBLANES - 2 + tm, :]
                    + cw[1:2, :] * u_ref[V7X_SUBLANES - 1:V7X_SUBLANES - 1 + tm, :]
                    + cw[2:3, :] * u)
        u_ref[0:V7X_SUBLANES, :] = u_ref[tm:tm + V7X_SUBLANES, :]

    def conv_gate():
        mix_ref[:, ATTN_WIDTH:] = (proj(KV_END, CB_END) * val["y"]).astype(_BF16)

    def out_proj(r0, r1):
        o_ref[r0:r1, :] = x_ref[r0:r1, :] + jnp.dot(mix_ref[r0:r1, :], wout_ref[...],
                                                    preferred_element_type=_F32)

    prev_mask = jnp.where(first, MASK_VALUE, 0.0).astype(_F32)
    lane = lax.broadcasted_iota(jnp.int32, (1, 2 * BLOCK), 1)
    nt = (((1,), (1,)), ((), ()))
    chains = [(j, g, half) for j in range(tm // BLOCK) for g in range(N_KV_HEADS)
              for half in range(2)]
    half_tile = tm // 2
    filler_fns = dict(q=step_q, kv=step_kv, conv_in=conv_in, conv_taps=conv_taps,
                      conv_gate=conv_gate, out_first=functools.partial(out_proj, 0, half_tile))
    for name in PRE_FILLERS:
        filler_fns[name]()
    fillers_a = {c: filler_fns[name] for c, name in FILLERS_A.items()}
    fillers_b = {c: filler_fns[name] for c, name in FILLERS_B.items()}

    def scores(j, g, half):
        rows = slice(j * BLOCK, (j + 1) * BLOCK)
        q = val["q"]
        qg = jnp.concatenate([q[rows, (2 * g) * V7X_LANES:(2 * g + 1) * V7X_LANES],
                              q[rows, (2 * g + 1) * V7X_LANES:(2 * g + 2) * V7X_LANES]], axis=0)
        return lax.dot_general(kp_ref[2 * half + g, j * BLOCK:(j + 2) * BLOCK, :], qg, nt,
                               preferred_element_type=_F32)

    stats = []
    for c, (j, g, half) in enumerate(chains):
        s = scores(j, g, half) + bias_ref[2 * g + half]
        if j == 0:
            s = jnp.concatenate([s[:BLOCK] + prev_mask, s[BLOCK:]], axis=0)
        s_ref[c] = s
        sink = LOG2E * jnp.where(lane < BLOCK, sink_ref[GQA_GROUP * g + half],
                                 sink_ref[GQA_GROUP * g + 2 + half])
        stats.append((jnp.maximum(jnp.max(s, axis=0, keepdims=True), sink), sink))
        if c in fillers_a:
            fillers_a[c]()

    outs = {}
    for c, (j, g, half) in enumerate(chains):
        if c in fillers_b:
            fillers_b[c]()
        m, sink = stats[c]
        p = jnp.exp2(s_ref[c] - m)
        denom = jnp.sum(p, axis=0, keepdims=True) + jnp.exp2(sink - m)
        o = jnp.dot(vt_ref[:, j * BLOCK:(j + 2) * BLOCK], p.astype(_BF16),
                    preferred_element_type=_F32)
        outs[half] = o[HEAD_DIM * g:HEAD_DIM * (g + 1), :] * (1.0 / denom)
        if half == 1:
            for pr in range(2):
                qcols = slice(pr * BLOCK, (pr + 1) * BLOCK)
                at = jnp.concatenate([outs[0][:, qcols], outs[1][:, qcols]], axis=0)
                mix_ref[j * BLOCK:(j + 1) * BLOCK,
                        (2 * g + pr) * V7X_LANES:(2 * g + pr + 1) * V7X_LANES] = at.T.astype(_BF16)
    out_proj(half_tile, tm)

    kp_ref[:, 0:BLOCK, :] = kp_ref[:, tm:tm + BLOCK, :]
    vt_ref[:, 0:BLOCK] = vt_ref[:, tm:tm + BLOCK]


def _mixer(x, gain, w_in, conv_w, sink, w_out, seq_len, *, layer):
    tokens = x.shape[0]
    tm = TOKEN_TILE
    assert seq_len % tm == 0 and tm % BLOCK == 0
    bias = jnp.asarray(_attn_bias_table())
    n_chains = (tm // BLOCK) * N_KV_HEADS * 2
    stage_in, stage_out = _stage_shape((D_MODEL, CH_END)), _stage_shape((MIX_WIDTH, D_MODEL))
    tile_bytes = 2 * tm * D_MODEL * 4
    resident = (D_MODEL * CH_END * 2 + MIX_WIDTH * D_MODEL * 2 + bias.size * 4
                + (D_MODEL + CONV_K * CONV_WIDTH) * 4
                + 4 * (math.prod(stage_in) + math.prod(stage_out)))
    scratch = ((4 + 1) * (BLOCK + tm) * V7X_LANES * 2 + (V7X_SUBLANES + tm) * CONV_WIDTH * 4
               + tm * MIX_WIDTH * 2 + n_chains * 4 * BLOCK * BLOCK * 4)
    temps = tm * D_MODEL * (2 + 4) + tm * CH_END * 4 + 8 * 1024 * 1024
    row = pl.BlockSpec((tm, D_MODEL), lambda i: (i, 0))
    return pl.pallas_call(
        functools.partial(_mixer_body, layer=layer, tiles_per_seq=seq_len // tm),
        grid=(tokens // tm,),
        in_specs=[pl.BlockSpec(memory_space=pltpu.SMEM), row, _resident((1, D_MODEL)),
                  _HBM, _resident((CONV_K, CONV_WIDTH)), _HBM, _resident(bias.shape)],
        out_specs=row,
        out_shape=jax.ShapeDtypeStruct(x.shape, _F32),
        scratch_shapes=[pltpu.VMEM((4, BLOCK + tm, V7X_LANES), _BF16),
                        pltpu.VMEM((KV_WIDTH, BLOCK + tm), _BF16),
                        pltpu.VMEM((V7X_SUBLANES + tm, CONV_WIDTH), _F32),
                        pltpu.VMEM((tm, MIX_WIDTH), _BF16),
                        pltpu.VMEM((n_chains, 2 * BLOCK, 2 * BLOCK), _F32),
                        pltpu.VMEM((D_MODEL, CH_END), _BF16), pltpu.VMEM((MIX_WIDTH, D_MODEL), _BF16),
                        pltpu.VMEM(stage_in, _F32), pltpu.VMEM(stage_out, _F32),
                        pltpu.SemaphoreType.DMA((2, WEIGHT_STAGE_SLOTS))],
        compiler_params=pltpu.CompilerParams(
            dimension_semantics=("arbitrary",),
            vmem_limit_bytes=_vmem_limit(resident + scratch, tile_bytes, temps)),
        name="mixer",
    )(sink, x, gain, w_in, conv_w, w_out, bias)


def kernel(x, ffn1_norm, ffn1_wg, ffn1_wu, ffn1_wd, mix_norm, w_in, conv_w, attn_sink, w_out,
           ffn2_norm, ffn2_wg, ffn2_wu, ffn2_wd, final_norm):
    bsz, seq, d = x.shape
    depth = w_in.shape[0]
    xt = x.reshape(bsz * seq, d)
    fgain = final_norm.reshape(1, d)
    for l in range(depth):
        xt = _ffn(xt, ffn1_norm[l].reshape(1, d), ffn1_wg, ffn1_wu, ffn1_wd, fgain,
                  layer=l, final_norm=False)
        xt = _mixer(xt, mix_norm[l].reshape(1, d), w_in, conv_w[l], attn_sink[l], w_out, seq,
                    layer=l)
        xt = _ffn(xt, ffn2_norm[l].reshape(1, d), ffn2_wg, ffn2_wu, ffn2_wd, fgain,
                  layer=l, final_norm=(l == depth - 1))
    return xt.reshape(bsz, seq, d)
```

```python
import functools
import math

import numpy as np
import jax
import jax.numpy as jnp
from jax import lax
from jax.experimental import pallas as pl
from jax.experimental.pallas import tpu as pltpu

D_MODEL = 1024
HEAD_DIM = 64
N_Q_HEADS = 8
N_KV_HEADS = 2
GQA_GROUP = N_Q_HEADS // N_KV_HEADS
ATTN_WIDTH = N_Q_HEADS * HEAD_DIM
KV_WIDTH = N_KV_HEADS * HEAD_DIM
CONV_WIDTH = 512
MIX_WIDTH = ATTN_WIDTH + CONV_WIDTH
CONV_K = 3
WINDOW = 128
BLOCK = 128
D_FF = 2816
NORM_EPS = 1e-6

Q_END = ATTN_WIDTH
KV_END = Q_END + 2 * KV_WIDTH
CB_END = KV_END + CONV_WIDTH
CC_END = CB_END + CONV_WIDTH
CH_END = CC_END + CONV_WIDTH

V7X_LANES = 128
V7X_SUBLANES = 8
V7X_MXU_DIM = 256
V7X_VMEM_BYTES = 64 * 1024 * 1024

TOKEN_TILE = 1024
MIXER_SUB_TILE = 512
FFN_TOKEN_TILE = 1024
FFN_SUB_TILE = 512
FFN_STAGGER = 6
FF_CHUNK = V7X_MXU_DIM
WEIGHT_STAGE_BYTES = 1536 * 1024
WEIGHT_STAGE_SLOTS = 3
MASK_VALUE = -1e30
LOG2E = math.log2(math.e)

_F32 = jnp.float32
_BF16 = jnp.bfloat16


def _vmem_limit(resident_bytes, tile_bytes, temp_bytes):
    need = resident_bytes + 2 * tile_bytes + temp_bytes
    assert need <= V7X_VMEM_BYTES, need
    return int(need)


def _rms_norm(x, gain):
    inv = lax.rsqrt(jnp.mean(x * x, axis=-1, keepdims=True) + NORM_EPS)
    return (x * inv) * gain


def _resident(shape):
    zeros = (0,) * len(shape)
    return pl.BlockSpec(shape, lambda i: zeros, pipeline_mode=pl.Buffered(1))


_HBM = pl.BlockSpec(memory_space=pl.ANY)


def _stage_shape(weight_shape):
    rows, cols = weight_shape
    bf16_rows = 2 * V7X_SUBLANES
    fits = [r for r in range(bf16_rows, WEIGHT_STAGE_BYTES // (cols * 4) + 1, bf16_rows)
            if rows % r == 0]
    return (WEIGHT_STAGE_SLOTS, max(fits), cols)


def _load_weights_bf16(jobs, stages, sem_ref):
    chunks, used = [], [0] * len(stages)
    for src, dst, si in jobs:
        slots, rows, _ = stages[si].shape
        for r0 in range(0, dst.shape[0], rows):
            chunks.append((src, dst, si, used[si] % slots, r0, rows))
            used[si] += 1

    def copy(chunk):
        src, _, si, slot, r0, rows = chunk
        return pltpu.make_async_copy(src.at[pl.ds(r0, rows), :], stages[si].at[slot],
                                     sem_ref.at[si, slot])

    ahead = min(stage.shape[0] for stage in stages) - 1
    for chunk in chunks[:ahead]:
        copy(chunk).start()
    for k, chunk in enumerate(chunks):
        if k + ahead < len(chunks):
            copy(chunks[k + ahead]).start()
        copy(chunk).wait()
        _, dst, si, slot, r0, rows = chunk
        dst[r0:r0 + rows, :] = stages[si][slot].astype(_BF16)


def _ffn_body(x_ref, gain_ref, wg_hbm, wu_hbm, wd_hbm, fgain_ref, o_ref,
              h_ref, wg_ref, wu_ref, wd_ref, stage_in_ref, stage_out_ref, sem_ref,
              *, layer, final_norm):
    @pl.when(pl.program_id(0) == 0)
    def _():
        _load_weights_bf16([(wg_hbm.at[layer], wg_ref, 0), (wu_hbm.at[layer], wu_ref, 0),
                            (wd_hbm.at[layer], wd_ref, 1)], [stage_in_ref, stage_out_ref], sem_ref)

    subs = [slice(r0, r0 + FFN_SUB_TILE) for r0 in range(0, x_ref.shape[0], FFN_SUB_TILE)]
    chunks = list(range(0, D_FF, FF_CHUNK))
    order = sorted((ci + si * FFN_STAGGER, si, ci)
                   for si in range(len(subs)) for ci in range(len(chunks)))
    acc = [None] * len(subs)
    for _, si, ci in order:
        rows, c0 = subs[si], chunks[ci]
        if ci == 0:
            h_ref[rows, :] = _rms_norm(x_ref[rows, :], gain_ref[...]).astype(_BF16)
        h = h_ref[rows, :]
        g = jnp.dot(h, wg_ref[:, c0:c0 + FF_CHUNK], preferred_element_type=_F32)
        u = jnp.dot(h, wu_ref[:, c0:c0 + FF_CHUNK], preferred_element_type=_F32)
        a = (g / (1.0 + jnp.exp(-g)) * u).astype(_BF16)
        part = jnp.dot(a, wd_ref[c0:c0 + FF_CHUNK, :], preferred_element_type=_F32)
        acc[si] = part if acc[si] is None else acc[si] + part
        if ci == len(chunks) - 1:
            y = x_ref[rows, :] + 0.5 * acc[si]
            if final_norm:
                y = _rms_norm(y, fgain_ref[...])
            o_ref[rows, :] = y


def _ffn(x, gain, wg, wu, wd, fgain, *, layer, final_norm):
    tokens = x.shape[0]
    tm = FFN_TOKEN_TILE
    assert tokens % tm == 0 and tm % FFN_SUB_TILE == 0 and D_FF % FF_CHUNK == 0
    stage_in, stage_out = _stage_shape((D_MODEL, D_FF)), _stage_shape((D_FF, D_MODEL))
    tile_bytes = 2 * tm * D_MODEL * 4
    resident = (3 * D_MODEL * D_FF * 2 + 2 * D_MODEL * 4
                + 4 * (math.prod(stage_in) + math.prod(stage_out)))
    temps = tm * D_MODEL * (2 + 4 + 4) + 3 * tm * FF_CHUNK * 4 * 2
    row = pl.BlockSpec((tm, D_MODEL), lambda i: (i, 0))
    return pl.pallas_call(
        functools.partial(_ffn_body, layer=layer, final_norm=final_norm),
        grid=(tokens // tm,),
        in_specs=[row, _resident((1, D_MODEL)), _HBM, _HBM, _HBM, _resident((1, D_MODEL))],
        out_specs=row,
        out_shape=jax.ShapeDtypeStruct(x.shape, _F32),
        scratch_shapes=[pltpu.VMEM((tm, D_MODEL), _BF16),
                        pltpu.VMEM((D_MODEL, D_FF), _BF16), pltpu.VMEM((D_MODEL, D_FF), _BF16),
                        pltpu.VMEM((D_FF, D_MODEL), _BF16),
                        pltpu.VMEM(stage_in, _F32), pltpu.VMEM(stage_out, _F32),
                        pltpu.SemaphoreType.DMA((2, WEIGHT_STAGE_SLOTS))],
        compiler_params=pltpu.CompilerParams(
            dimension_semantics=("arbitrary",),
            vmem_limit_bytes=_vmem_limit(resident, tile_bytes, temps)),
        name="ffn_final" if final_norm else "ffn",
    )(x, gain, wg, wu, wd, fgain)


def _attn_bias_table():
    kj = np.arange(2 * BLOCK)[:, None]
    qi = np.arange(BLOCK)[None, :]
    dist = (qi + BLOCK - kj).astype(np.float32)
    in_window = (dist >= 0) & (dist < WINDOW)
    slopes = np.exp2(-8.0 * np.arange(1, N_Q_HEADS + 1, dtype=np.float32) / N_Q_HEADS)
    per_head = np.where(in_window[None], -slopes[:, None, None] * dist[None] * np.float32(LOG2E),
                        np.float32(MASK_VALUE))
    table = [np.concatenate([per_head[GQA_GROUP * g + half], per_head[GQA_GROUP * g + 2 + half]], axis=1)
             for g in range(N_KV_HEADS) for half in range(2)]
    return np.stack(table).astype(np.float32)


def _place_heads(t):
    lo = lax.broadcasted_iota(jnp.int32, t.shape, 1) < HEAD_DIM
    r = pltpu.roll(t, HEAD_DIM, axis=1)
    zero = jnp.zeros_like(t)
    placed = (jnp.where(lo, t, zero), jnp.where(lo, r, zero),
              jnp.where(lo, zero, r), jnp.where(lo, zero, t))
    return [p.astype(_BF16) for p in placed]


def _mixer_body(sink_ref, x_ref, gain_ref, win_hbm, convw_ref, wout_hbm, bias_ref, o_ref,
                kp_ref, vt_ref, u_ref, mix_ref, s_ref, h_ref,
                win_ref, wout_ref, stage_in_ref, stage_out_ref, sem_ref, *, layer, tiles_per_seq):
    tm = x_ref.shape[0]
    first = pl.program_id(0) % tiles_per_seq == 0

    @pl.when(pl.program_id(0) == 0)
    def _():
        _load_weights_bf16([(win_hbm.at[layer], win_ref, 0), (wout_hbm.at[layer], wout_ref, 1)],
                           [stage_in_ref, stage_out_ref], sem_ref)

    @pl.when(first)
    def _():
        kp_ref[:, 0:BLOCK, :] = jnp.zeros((4, BLOCK, V7X_LANES), _BF16)
        vt_ref[:, 0:BLOCK] = jnp.zeros((KV_WIDTH, BLOCK), _BF16)
        u_ref[0:V7X_SUBLANES, :] = jnp.zeros((V7X_SUBLANES, CONV_WIDTH), _F32)

    sub = MIXER_SUB_TILE
    n_sub = tm // sub
    val = {}

    def rows(s, lo=0, hi=sub):
        return slice(s * sub + lo, s * sub + hi)

    def proj(s, c0, c1):
        return jnp.dot(h_ref[rows(s), :], win_ref[:, c0:c1], preferred_element_type=_F32)

    def conv_in(s):
        h_ref[rows(s), :] = _rms_norm(x_ref[rows(s), :], gain_ref[...]).astype(_BF16)
        val["cc", s] = proj(s, CB_END, CC_END)

    def conv_taps(s):
        u = val.pop(("cc", s)) * proj(s, CC_END, CH_END)
        base = V7X_SUBLANES + s * sub
        u_ref[base:base + sub, :] = u
        cw = convw_ref[...]
        val["y", s] = (cw[0:1, :] * u_ref[base - 2:base - 2 + sub, :]
                       + cw[1:2, :] * u_ref[base - 1:base - 1 + sub, :]
                       + cw[2:3, :] * u)

    def step_q(s):
        val["q", s] = (proj(s, 0, Q_END) * (LOG2E / math.sqrt(HEAD_DIM))).astype(_BF16)

    def step_kv(s):
        kv = proj(s, Q_END, KV_END)
        band = slice(BLOCK + s * sub, BLOCK + (s + 1) * sub)
        for idx, p in enumerate(_place_heads(kv[:, :KV_WIDTH])):
            kp_ref[idx, band, :] = p
        vt_ref[:, band] = kv[:, KV_WIDTH:].T.astype(_BF16)

    def conv_gate(s):
        mix_ref[rows(s), ATTN_WIDTH:] = (proj(s, KV_END, CB_END) * val.pop(("y", s))).astype(_BF16)

    projection_steps = (conv_in, conv_taps, step_q, step_kv, conv_gate)

    def out_proj(s, lo, hi):
        r = rows(s, lo, hi)
        o_ref[r, :] = x_ref[r, :] + jnp.dot(mix_ref[r, :], wout_ref[...],
                                            preferred_element_type=_F32)

    prev_mask = jnp.where(first, MASK_VALUE, 0.0).astype(_F32)
    lane = lax.broadcasted_iota(jnp.int32, (1, 2 * BLOCK), 1)
    nt = (((1,), (1,)), ((), ()))
    chains = [(j, g, half) for j in range(sub // BLOCK) for g in range(N_KV_HEADS)
              for half in range(2)]

    def scores(s, j, g, half):
        q = val["q", s]
        qrows = slice(j * BLOCK, (j + 1) * BLOCK)
        qg = jnp.concatenate([q[qrows, (2 * g) * V7X_LANES:(2 * g + 1) * V7X_LANES],
                              q[qrows, (2 * g + 1) * V7X_LANES:(2 * g + 2) * V7X_LANES]], axis=0)
        band = slice(s * sub + j * BLOCK, s * sub + (j + 2) * BLOCK)
        return lax.dot_general(kp_ref[2 * half + g, band, :], qg, nt,
                               preferred_element_type=_F32)

    def score_phase(s):
        stats = []
        for c, (j, g, half) in enumerate(chains):
            sc = scores(s, j, g, half) + bias_ref[2 * g + half]
            if s == 0 and j == 0:
                sc = jnp.concatenate([sc[:BLOCK] + prev_mask, sc[BLOCK:]], axis=0)
            s_ref[c] = sc
            sink = LOG2E * jnp.where(lane < BLOCK, sink_ref[GQA_GROUP * g + half],
                                     sink_ref[GQA_GROUP * g + 2 + half])
            stats.append((jnp.maximum(jnp.max(sc, axis=0, keepdims=True), sink), sink))
        val.pop(("q", s))
        return stats

    def value_phase(s, stats, fillers):
        outs = {}
        for c, (j, g, half) in enumerate(chains):
            if c in fillers:
                fillers[c]()
            m, sink = stats[c]
            p = jnp.exp2(s_ref[c] - m)
            denom = jnp.sum(p, axis=0, keepdims=True) + jnp.exp2(sink - m)
            band = slice(s * sub + j * BLOCK, s * sub + (j + 2) * BLOCK)
            o = jnp.dot(vt_ref[:, band], p.astype(_BF16), preferred_element_type=_F32)
            outs[half] = o[HEAD_DIM * g:HEAD_DIM * (g + 1), :] * (1.0 / denom)
            if half == 1:
                for pr in range(2):
                    qcols = slice(pr * BLOCK, (pr + 1) * BLOCK)
                    at = jnp.concatenate([outs[0][:, qcols], outs[1][:, qcols]], axis=0)
                    mix_ref[rows(s, j * BLOCK, (j + 1) * BLOCK),
                            (2 * g + pr) * V7X_LANES:(2 * g + pr + 1) * V7X_LANES] = (
                                at.T.astype(_BF16))

    def spread(thunks, first, last):
        return {first + (i * (last - first)) // len(thunks): t for i, t in enumerate(thunks)}

    for step in projection_steps:
        step(0)
    pending = []
    for s in range(n_sub):
        stats = score_phase(s)
        own_half_done = len(chains) // 2 + 1
        if s + 1 < n_sub:
            fillers = spread(pending + [functools.partial(step, s + 1) for step in projection_steps],
                             0, len(chains))
            pending = [functools.partial(out_proj, s, 0, sub // 2),
                       functools.partial(out_proj, s, sub // 2, sub)]
        else:
            fillers = spread(pending, 0, own_half_done) if pending else {}
            fillers[own_half_done] = functools.partial(out_proj, s, 0, sub // 2)
            pending = [functools.partial(out_proj, s, sub // 2, sub)]
        value_phase(s, stats, fillers)
    for thunk in pending:
        thunk()

    u_ref[0:V7X_SUBLANES, :] = u_ref[tm:tm + V7X_SUBLANES, :]
    kp_ref[:, 0:BLOCK, :] = kp_ref[:, tm:tm + BLOCK, :]
    vt_ref[:, 0:BLOCK] = vt_ref[:, tm:tm + BLOCK]


def _mixer(x, gain, w_in, conv_w, sink, w_out, seq_len, *, layer):
    tokens = x.shape[0]
    tm = TOKEN_TILE
    sub = MIXER_SUB_TILE
    assert seq_len % tm == 0 and tm % sub == 0 and sub % BLOCK == 0
    bias = jnp.asarray(_attn_bias_table())
    n_chains = (sub // BLOCK) * N_KV_HEADS * 2
    stage_in, stage_out = _stage_shape((D_MODEL, CH_END)), _stage_shape((MIX_WIDTH, D_MODEL))
    tile_bytes = 2 * tm * D_MODEL * 4
    resident = (D_MODEL * CH_END * 2 + MIX_WIDTH * D_MODEL * 2 + bias.size * 4
                + (D_MODEL + CONV_K * CONV_WIDTH) * 4
                + 4 * (math.prod(stage_in) + math.prod(stage_out)))
    scratch = ((4 + 1) * (BLOCK + tm) * V7X_LANES * 2 + (V7X_SUBLANES + tm) * CONV_WIDTH * 4
               + tm * MIX_WIDTH * 2 + n_chains * 4 * BLOCK * BLOCK * 4 + tm * D_MODEL * 2)
    temps = sub * CH_END * 4 + 8 * 1024 * 1024
    row = pl.BlockSpec((tm, D_MODEL), lambda i: (i, 0))
    return pl.pallas_call(
        functools.partial(_mixer_body, layer=layer, tiles_per_seq=seq_len // tm),
        grid=(tokens // tm,),
        in_specs=[pl.BlockSpec(memory_space=pltpu.SMEM), row, _resident((1, D_MODEL)),
                  _HBM, _resident((CONV_K, CONV_WIDTH)), _HBM, _resident(bias.shape)],
        out_specs=row,
        out_shape=jax.ShapeDtypeStruct(x.shape, _F32),
        scratch_shapes=[pltpu.VMEM((4, BLOCK + tm, V7X_LANES), _BF16),
                        pltpu.VMEM((KV_WIDTH, BLOCK + tm), _BF16),
                        pltpu.VMEM((V7X_SUBLANES + tm, CONV_WIDTH), _F32),
                        pltpu.VMEM((tm, MIX_WIDTH), _BF16),
                        pltpu.VMEM((n_chains, 2 * BLOCK, 2 * BLOCK), _F32),
                        pltpu.VMEM((tm, D_MODEL), _BF16),
                        pltpu.VMEM((D_MODEL, CH_END), _BF16), pltpu.VMEM((MIX_WIDTH, D_MODEL), _BF16),
                        pltpu.VMEM(stage_in, _F32), pltpu.VMEM(stage_out, _F32),
                        pltpu.SemaphoreType.DMA((2, WEIGHT_STAGE_SLOTS))],
        compiler_params=pltpu.CompilerParams(
            dimension_semantics=("arbitrary",),
            vmem_limit_bytes=_vmem_limit(resident + scratch, tile_bytes, temps)),
        name="mixer",
    )(sink, x, gain, w_in, conv_w, w_out, bias)


def kernel(x, ffn1_norm, ffn1_wg, ffn1_wu, ffn1_wd, mix_norm, w_in, conv_w, attn_sink, w_out,
           ffn2_norm, ffn2_wg, ffn2_wu, ffn2_wd, final_norm):
    bsz, seq, d = x.shape
    depth = w_in.shape[0]
    xt = x.reshape(bsz * seq, d)
    fgain = final_norm.reshape(1, d)
    for l in range(depth):
        xt = _ffn(xt, ffn1_norm[l].reshape(1, d), ffn1_wg, ffn1_wu, ffn1_wd, fgain,
                  layer=l, final_norm=False)
        xt = _mixer(xt, mix_norm[l].reshape(1, d), w_in, conv_w[l], attn_sink[l], w_out, seq,
                    layer=l)
        xt = _ffn(xt, ffn2_norm[l].reshape(1, d), ffn2_wg, ffn2_wu, ffn2_wd, fgain,
                  layer=l, final_norm=(l == depth - 1))
    return xt.reshape(bsz, seq, d)
```

```python
import functools
import math

import numpy as np
import jax
import jax.numpy as jnp
from jax import lax
from jax.experimental import pallas as pl
from jax.experimental.pallas import tpu as pltpu

D_MODEL = 1024
HEAD_DIM = 64
N_Q_HEADS = 8
N_KV_HEADS = 2
GQA_GROUP = N_Q_HEADS // N_KV_HEADS
ATTN_WIDTH = N_Q_HEADS * HEAD_DIM
KV_WIDTH = N_KV_HEADS * HEAD_DIM
CONV_WIDTH = 512
MIX_WIDTH = ATTN_WIDTH + CONV_WIDTH
CONV_K = 3
WINDOW = 128
BLOCK = 128
D_FF = 2816
NORM_EPS = 1e-6

Q_END = ATTN_WIDTH
KV_END = Q_END + 2 * KV_WIDTH
CB_END = KV_END + CONV_WIDTH
CC_END = CB_END + CONV_WIDTH
CH_END = CC_END + CONV_WIDTH

V7X_LANES = 128
V7X_SUBLANES = 8
V7X_MXU_DIM = 256
V7X_VMEM_BYTES = 64 * 1024 * 1024

TOKEN_TILE = 1024
MIXER_SUB_TILE = 512
FFN_TOKEN_TILE = 1024
FFN_SUB_TILE = 512
FFN_STAGGER = 6
FFN_ANCHOR_BEFORE_END = 2
FF_CHUNKS = (V7X_MXU_DIM,) * (D_FF // V7X_MXU_DIM)
WEIGHT_STAGE_BYTES = 1536 * 1024
WEIGHT_STAGE_SLOTS = 3
MASK_VALUE = -1e30
LOG2E = math.log2(math.e)

_F32 = jnp.float32
_BF16 = jnp.bfloat16


def _vmem_limit(resident_bytes, tile_bytes, temp_bytes):
    need = resident_bytes + 2 * tile_bytes + temp_bytes
    assert need <= V7X_VMEM_BYTES, need
    return int(need)


def _rms_norm(x, gain):
    inv = lax.rsqrt(jnp.mean(x * x, axis=-1, keepdims=True) + NORM_EPS)
    return (x * inv) * gain


def _schedule_anchor(values):
    rows, cols = values.shape
    t = values[0:V7X_SUBLANES, :]
    for r0 in range(V7X_SUBLANES, rows, V7X_SUBLANES):
        t = jnp.maximum(t, values[r0:r0 + V7X_SUBLANES, :])
    acc = t[:, 0:V7X_LANES]
    for c0 in range(V7X_LANES, cols, V7X_LANES):
        acc = jnp.maximum(acc, t[:, c0:c0 + V7X_LANES])
    bits = pltpu.bitcast(acc, jnp.uint32)
    return pltpu.bitcast((bits >> 16) >> 16, _F32)


def _add_to_first_tile(array, tile):
    top = jnp.concatenate([array[0:V7X_SUBLANES, 0:V7X_LANES] + tile,
                           array[0:V7X_SUBLANES, V7X_LANES:]], axis=1)
    return jnp.concatenate([top, array[V7X_SUBLANES:, :]], axis=0)


def _resident(shape):
    zeros = (0,) * len(shape)
    return pl.BlockSpec(shape, lambda i: zeros, pipeline_mode=pl.Buffered(1))


_HBM = pl.BlockSpec(memory_space=pl.ANY)


def _next_first_sub_tile(tokens, tile, sub):
    last = tokens // sub - 1
    return pl.BlockSpec((sub, D_MODEL), lambda i: (jnp.minimum((i + 1) * (tile // sub), last), 0))


def _stage_shape(weight_shape):
    rows, cols = weight_shape
    bf16_rows = 2 * V7X_SUBLANES
    fits = [r for r in range(bf16_rows, WEIGHT_STAGE_BYTES // (cols * 4) + 1, bf16_rows)
            if rows % r == 0]
    return (WEIGHT_STAGE_SLOTS, max(fits), cols)


def _load_weights_bf16(jobs, stages, sem_ref):
    chunks, used = [], [0] * len(stages)
    for src, dst, si in jobs:
        slots, rows, _ = stages[si].shape
        for r0 in range(0, dst.shape[0], rows):
            chunks.append((src, dst, si, used[si] % slots, r0, rows))
            used[si] += 1

    def copy(chunk):
        src, _, si, slot, r0, rows = chunk
        return pltpu.make_async_copy(src.at[pl.ds(r0, rows), :], stages[si].at[slot],
                                     sem_ref.at[si, slot])

    ahead = min(stage.shape[0] for stage in stages) - 1
    for chunk in chunks[:ahead]:
        copy(chunk).start()
    for k, chunk in enumerate(chunks):
        if k + ahead < len(chunks):
            copy(chunks[k + ahead]).start()
        copy(chunk).wait()
        _, dst, si, slot, r0, rows = chunk
        dst[r0:r0 + rows, :] = stages[si][slot].astype(_BF16)


def _ffn_body(x_ref, xnext_ref, gain_ref, wg_hbm, wu_hbm, wd_hbm, fgain_ref, o_ref,
              h_ref, wg_ref, wu_ref, wd_ref, stage_in_ref, stage_out_ref, sem_ref,
              *, layer, final_norm):
    subs = [slice(r0, r0 + FFN_SUB_TILE) for r0 in range(0, x_ref.shape[0], FFN_SUB_TILE)]

    def norm_into(rows, src):
        normed = _rms_norm(src, gain_ref[...])
        h_ref[rows, :] = normed.astype(_BF16)
        return normed

    @pl.when(pl.program_id(0) == 0)
    def _():
        _load_weights_bf16([(wg_hbm.at[layer], wg_ref, 0), (wu_hbm.at[layer], wu_ref, 0),
                            (wd_hbm.at[layer], wd_ref, 1)], [stage_in_ref, stage_out_ref], sem_ref)
        norm_into(subs[0], x_ref[subs[0], :])

    anchor = {}

    chunks = [(sum(FF_CHUNKS[:i]), sum(FF_CHUNKS[:i + 1])) for i in range(len(FF_CHUNKS))]
    order = sorted((ci + si * FFN_STAGGER, si, ci)
                   for si in range(len(subs)) for ci in range(len(chunks)))
    acc = [None] * len(subs)
    for _, si, ci in order:
        rows, (c0, c1) = subs[si], chunks[ci]
        if ci == 0 and si > 0:
            norm_into(rows, x_ref[rows, :])
        h = h_ref[rows, :]
        g = jnp.dot(h, wg_ref[:, c0:c1], preferred_element_type=_F32)
        u = jnp.dot(h, wu_ref[:, c0:c1], preferred_element_type=_F32)
        if anchor and ci == len(chunks) - FFN_ANCHOR_BEFORE_END:
            u = _add_to_first_tile(u, anchor.pop("next_norm"))
        a = (g / (1.0 + jnp.exp(-g)) * u).astype(_BF16)
        part = jnp.dot(a, wd_ref[c0:c1, :], preferred_element_type=_F32)
        acc[si] = part if acc[si] is None else acc[si] + part
        if ci == len(chunks) - 1:
            y = x_ref[rows, :] + 0.5 * acc[si]
            if final_norm:
                y = _rms_norm(y, fgain_ref[...])
            o_ref[rows, :] = y
            if si == 0:
                anchor["next_norm"] = _schedule_anchor(norm_into(rows, xnext_ref[...]))


def _ffn(x, gain, wg, wu, wd, fgain, *, layer, final_norm):
    tokens = x.shape[0]
    tm = FFN_TOKEN_TILE
    assert tokens % tm == 0 and tm % FFN_SUB_TILE == 0 and sum(FF_CHUNKS) == D_FF
    stage_in, stage_out = _stage_shape((D_MODEL, D_FF)), _stage_shape((D_FF, D_MODEL))
    tile_bytes = (2 * tm + FFN_SUB_TILE) * D_MODEL * 4
    resident = (3 * D_MODEL * D_FF * 2 + 2 * D_MODEL * 4
                + 4 * (math.prod(stage_in) + math.prod(stage_out)))
    temps = tm * D_MODEL * (2 + 4 + 4) + 3 * tm * max(FF_CHUNKS) * 4 * 2
    row = pl.BlockSpec((tm, D_MODEL), lambda i: (i, 0))
    return pl.pallas_call(
        functools.partial(_ffn_body, layer=layer, final_norm=final_norm),
        grid=(tokens // tm,),
        in_specs=[row, _next_first_sub_tile(tokens, tm, FFN_SUB_TILE), _resident((1, D_MODEL)),
                  _HBM, _HBM, _HBM, _resident((1, D_MODEL))],
        out_specs=row,
        out_shape=jax.ShapeDtypeStruct(x.shape, _F32),
        scratch_shapes=[pltpu.VMEM((tm, D_MODEL), _BF16),
                        pltpu.VMEM((D_MODEL, D_FF), _BF16), pltpu.VMEM((D_MODEL, D_FF), _BF16),
                        pltpu.VMEM((D_FF, D_MODEL), _BF16),
                        pltpu.VMEM(stage_in, _F32), pltpu.VMEM(stage_out, _F32),
                        pltpu.SemaphoreType.DMA((2, WEIGHT_STAGE_SLOTS))],
        compiler_params=pltpu.CompilerParams(
            dimension_semantics=("arbitrary",),
            vmem_limit_bytes=_vmem_limit(resident, tile_bytes, temps)),
        name="ffn_final" if final_norm else "ffn",
    )(x, x, gain, wg, wu, wd, fgain)


def _attn_bias_table():
    kj = np.arange(2 * BLOCK)[:, None]
    qi = np.arange(BLOCK)[None, :]
    dist = (qi + BLOCK - kj).astype(np.float32)
    in_window = (dist >= 0) & (dist < WINDOW)
    slopes = np.exp2(-8.0 * np.arange(1, N_Q_HEADS + 1, dtype=np.float32) / N_Q_HEADS)
    per_head = np.where(in_window[None], -slopes[:, None, None] * dist[None] * np.float32(LOG2E),
                        np.float32(MASK_VALUE))
    table = [np.concatenate([per_head[GQA_GROUP * g + half], per_head[GQA_GROUP * g + 2 + half]], axis=1)
             for g in range(N_KV_HEADS) for half in range(2)]
    return np.stack(table).astype(np.float32)


def _place_heads(t):
    lo = lax.broadcasted_iota(jnp.int32, t.shape, 1) < HEAD_DIM
    r = pltpu.roll(t, HEAD_DIM, axis=1)
    zero = jnp.zeros_like(t)
    placed = (jnp.where(lo, t, zero), jnp.where(lo, r, zero),
              jnp.where(lo, zero, r), jnp.where(lo, zero, t))
    return [p.astype(_BF16) for p in placed]


def _mixer_body(sink_ref, x_ref, gain_ref, win_hbm, convw_ref, wout_hbm, bias_ref,
                o_ref, kp_ref, vt_ref, u_ref, mix_ref, s_ref, h_ref,
                win_ref, wout_ref, stage_in_ref, stage_out_ref, sem_ref, *, layer, tiles_per_seq):
    tm = x_ref.shape[0]
    first = pl.program_id(0) % tiles_per_seq == 0

    @pl.when(pl.program_id(0) == 0)
    def _():
        _load_weights_bf16([(win_hbm.at[layer], win_ref, 0), (wout_hbm.at[layer], wout_ref, 1)],
                           [stage_in_ref, stage_out_ref], sem_ref)

    @pl.when(first)
    def _():
        kp_ref[:, 0:BLOCK, :] = jnp.zeros((4, BLOCK, V7X_LANES), _BF16)
        vt_ref[:, 0:BLOCK] = jnp.zeros((KV_WIDTH, BLOCK), _BF16)
        u_ref[0:V7X_SUBLANES, :] = jnp.zeros((V7X_SUBLANES, CONV_WIDTH), _F32)

    sub = MIXER_SUB_TILE
    n_sub = tm // sub
    val = {}

    def rows(s, lo=0, hi=sub):
        return slice(s * sub + lo, s * sub + hi)

    def proj(s, c0, c1):
        return jnp.dot(h_ref[rows(s), :], win_ref[:, c0:c1], preferred_element_type=_F32)

    def conv_in(s):
        h_ref[rows(s), :] = _rms_norm(x_ref[rows(s), :], gain_ref[...]).astype(_BF16)
        val["cc", s] = proj(s, CB_END, CC_END)

    def conv_taps(s):
        u = val.pop(("cc", s)) * proj(s, CC_END, CH_END)
        base = V7X_SUBLANES + s * sub
        u_ref[base:base + sub, :] = u
        cw = convw_ref[...]
        val["y", s] = (cw[0:1, :] * u_ref[base - 2:base - 2 + sub, :]
                       + cw[1:2, :] * u_ref[base - 1:base - 1 + sub, :]
                       + cw[2:3, :] * u)

    def step_q(s):
        val["q", s] = (proj(s, 0, Q_END) * (LOG2E / math.sqrt(HEAD_DIM))).astype(_BF16)

    def step_kv(s):
        kv = proj(s, Q_END, KV_END)
        band = slice(BLOCK + s * sub, BLOCK + (s + 1) * sub)
        for idx, p in enumerate(_place_heads(kv[:, :KV_WIDTH])):
            kp_ref[idx, band, :] = p
        vt_ref[:, band] = kv[:, KV_WIDTH:].T.astype(_BF16)

    def conv_gate(s):
        mix_ref[rows(s), ATTN_WIDTH:] = (proj(s, KV_END, CB_END) * val.pop(("y", s))).astype(_BF16)

    projection_steps = (conv_in, conv_taps, step_q, step_kv, conv_gate)

    def out_proj(s, lo, hi):
        r = rows(s, lo, hi)
        o_ref[r, :] = x_ref[r, :] + jnp.dot(mix_ref[r, :], wout_ref[...],
                                            preferred_element_type=_F32)

    prev_mask = jnp.where(first, MASK_VALUE, 0.0).astype(_F32)
    lane = lax.broadcasted_iota(jnp.int32, (1, 2 * BLOCK), 1)
    nt = (((1,), (1,)), ((), ()))
    chains = [(j, g, half) for j in range(sub // BLOCK) for g in range(N_KV_HEADS)
              for half in range(2)]

    def scores(s, j, g, half):
        q = val["q", s]
        qrows = slice(j * BLOCK, (j + 1) * BLOCK)
        qg = jnp.concatenate([q[qrows, (2 * g) * V7X_LANES:(2 * g + 1) * V7X_LANES],
                              q[qrows, (2 * g + 1) * V7X_LANES:(2 * g + 2) * V7X_LANES]], axis=0)
        band = slice(s * sub + j * BLOCK, s * sub + (j + 2) * BLOCK)
        return lax.dot_general(kp_ref[2 * half + g, band, :], qg, nt,
                               preferred_element_type=_F32)

    def score_phase(s):
        stats = []
        for c, (j, g, half) in enumerate(chains):
            sc = scores(s, j, g, half) + bias_ref[2 * g + half]
            if s == 0 and j == 0:
                sc = jnp.concatenate([sc[:BLOCK] + prev_mask, sc[BLOCK:]], axis=0)
            s_ref[c] = sc
            sink = LOG2E * jnp.where(lane < BLOCK, sink_ref[GQA_GROUP * g + half],
                                     sink_ref[GQA_GROUP * g + 2 + half])
            stats.append((jnp.maximum(jnp.max(sc, axis=0, keepdims=True), sink), sink))
        val.pop(("q", s))
        return stats

    def value_phase(s, stats, fillers):
        outs = {}
        for c, (j, g, half) in enumerate(chains):
            if c in fillers:
                fillers[c]()
            m, sink = stats[c]
            p = jnp.exp2(s_ref[c] - m)
            denom = jnp.sum(p, axis=0, keepdims=True) + jnp.exp2(sink - m)
            band = slice(s * sub + j * BLOCK, s * sub + (j + 2) * BLOCK)
            o = jnp.dot(vt_ref[:, band], p.astype(_BF16), preferred_element_type=_F32)
            outs[half] = o[HEAD_DIM * g:HEAD_DIM * (g + 1), :] * (1.0 / denom)
            if half == 1:
                for pr in range(2):
                    qcols = slice(pr * BLOCK, (pr + 1) * BLOCK)
                    at = jnp.concatenate([outs[0][:, qcols], outs[1][:, qcols]], axis=0)
                    mix_ref[rows(s, j * BLOCK, (j + 1) * BLOCK),
                            (2 * g + pr) * V7X_LANES:(2 * g + pr + 1) * V7X_LANES] = (
                                at.T.astype(_BF16))

    def spread(thunks, first, last):
        return {first + (i * (last - first)) // len(thunks): t for i, t in enumerate(thunks)}

    for step in projection_steps:
        step(0)
    pending = []
    for s in range(n_sub):
        stats = score_phase(s)
        own_half_done = len(chains) // 2 + 1
        if s + 1 < n_sub:
            fillers = spread(pending + [functools.partial(step, s + 1) for step in projection_steps],
                             0, len(chains))
            pending = [functools.partial(out_proj, s, 0, sub // 2),
                       functools.partial(out_proj, s, sub // 2, sub)]
        else:
            fillers = spread(pending, 0, own_half_done) if pending else {}
            fillers[own_half_done] = functools.partial(out_proj, s, 0, sub // 2)
            pending = [functools.partial(out_proj, s, sub // 2, sub)]
        value_phase(s, stats, fillers)
    for thunk in pending:
        thunk()

    u_ref[0:V7X_SUBLANES, :] = u_ref[tm:tm + V7X_SUBLANES, :]
    kp_ref[:, 0:BLOCK, :] = kp_ref[:, tm:tm + BLOCK, :]
    vt_ref[:, 0:BLOCK] = vt_ref[:, tm:tm + BLOCK]


def _mixer(x, gain, w_in, conv_w, sink, w_out, seq_len, *, layer):
    tokens = x.shape[0]
    tm = TOKEN_TILE
    sub = MIXER_SUB_TILE
    assert seq_len % tm == 0 and tm % sub == 0 and sub % BLOCK == 0
    bias = jnp.asarray(_attn_bias_table())
    n_chains = (sub // BLOCK) * N_KV_HEADS * 2
    stage_in, stage_out = _stage_shape((D_MODEL, CH_END)), _stage_shape((MIX_WIDTH, D_MODEL))
    tile_bytes = 2 * tm * D_MODEL * 4
    resident = (D_MODEL * CH_END * 2 + MIX_WIDTH * D_MODEL * 2 + bias.size * 4
                + (D_MODEL + CONV_K * CONV_WIDTH) * 4
                + 4 * (math.prod(stage_in) + math.prod(stage_out)))
    scratch = ((4 + 1) * (BLOCK + tm) * V7X_LANES * 2 + (V7X_SUBLANES + tm) * CONV_WIDTH * 4
               + tm * MIX_WIDTH * 2 + n_chains * 4 * BLOCK * BLOCK * 4 + tm * D_MODEL * 2)
    temps = sub * CH_END * 4 + 8 * 1024 * 1024
    row = pl.BlockSpec((tm, D_MODEL), lambda i: (i, 0))
    return pl.pallas_call(
        functools.partial(_mixer_body, layer=layer, tiles_per_seq=seq_len // tm),
        grid=(tokens // tm,),
        in_specs=[pl.BlockSpec(memory_space=pltpu.SMEM), row, _resident((1, D_MODEL)),
                  _HBM, _resident((CONV_K, CONV_WIDTH)), _HBM, _resident(bias.shape)],
        out_specs=row,
        out_shape=jax.ShapeDtypeStruct(x.shape, _F32),
        scratch_shapes=[pltpu.VMEM((4, BLOCK + tm, V7X_LANES), _BF16),
                        pltpu.VMEM((KV_WIDTH, BLOCK + tm), _BF16),
                        pltpu.VMEM((V7X_SUBLANES + tm, CONV_WIDTH), _F32),
                        pltpu.VMEM((tm, MIX_WIDTH), _BF16),
                        pltpu.VMEM((n_chains, 2 * BLOCK, 2 * BLOCK), _F32),
                        pltpu.VMEM((tm, D_MODEL), _BF16),
                        pltpu.VMEM((D_MODEL, CH_END), _BF16), pltpu.VMEM((MIX_WIDTH, D_MODEL), _BF16),
                        pltpu.VMEM(stage_in, _F32), pltpu.VMEM(stage_out, _F32),
                        pltpu.SemaphoreType.DMA((2, WEIGHT_STAGE_SLOTS))],
        compiler_params=pltpu.CompilerParams(
            dimension_semantics=("arbitrary",),
            vmem_limit_bytes=_vmem_limit(resident + scratch, tile_bytes, temps)),
        name="mixer",
    )(sink, x, gain, w_in, conv_w, w_out, bias)


def kernel(x, ffn1_norm, ffn1_wg, ffn1_wu, ffn1_wd, mix_norm, w_in, conv_w, attn_sink, w_out,
           ffn2_norm, ffn2_wg, ffn2_wu, ffn2_wd, final_norm):
    bsz, seq, d = x.shape
    depth = w_in.shape[0]
    xt = x.reshape(bsz * seq, d)
    fgain = final_norm.reshape(1, d)
    for l in range(depth):
        xt = _ffn(xt, ffn1_norm[l].reshape(1, d), ffn1_wg, ffn1_wu, ffn1_wd, fgain,
                  layer=l, final_norm=False)
        xt = _mixer(xt, mix_norm[l].reshape(1, d), w_in, conv_w[l], attn_sink[l], w_out, seq,
                    layer=l)
        xt = _ffn(xt, ffn2_norm[l].reshape(1, d), ffn2_wg, ffn2_wu, ffn2_wd, fgain,
                  layer=l, final_norm=(l == depth - 1))
    return xt.reshape(bsz, seq, d)
```

```python
import functools
import math

import numpy as np
import jax
import jax.numpy as jnp
from jax import lax
from jax.experimental import pallas as pl
from jax.experimental.pallas import tpu as pltpu

D_MODEL = 1024
HEAD_DIM = 64
N_Q_HEADS = 8
N_KV_HEADS = 2
GQA_GROUP = N_Q_HEADS // N_KV_HEADS
ATTN_WIDTH = N_Q_HEADS * HEAD_DIM
KV_WIDTH = N_KV_HEADS * HEAD_DIM
CONV_WIDTH = 512
MIX_WIDTH = ATTN_WIDTH + CONV_WIDTH
CONV_K = 3
WINDOW = 128
BLOCK = 128
D_FF = 2816
NORM_EPS = 1e-6

Q_END = ATTN_WIDTH
KV_END = Q_END + 2 * KV_WIDTH
CB_END = KV_END + CONV_WIDTH
CC_END = CB_END + CONV_WIDTH
CH_END = CC_END + CONV_WIDTH

V7X_LANES = 128
V7X_SUBLANES = 8
V7X_MXU_DIM = 256
V7X_VMEM_BYTES = 64 * 1024 * 1024

TOKEN_TILE = 1024
MIXER_SUB_TILE = 512
FFN_TOKEN_TILE = 1024
FFN_SUB_TILE = 512
FFN_OUT_BLOCK = V7X_MXU_DIM
FF_CHUNKS = (V7X_MXU_DIM,) * (D_FF // V7X_MXU_DIM)
WEIGHT_STAGE_BYTES = 1536 * 1024
WEIGHT_STAGE_SLOTS = 3
MASK_VALUE = -1e30
LOG2E = math.log2(math.e)

_F32 = jnp.float32
_BF16 = jnp.bfloat16


def _vmem_limit(resident_bytes, tile_bytes, temp_bytes):
    need = resident_bytes + 2 * tile_bytes + temp_bytes
    assert need <= V7X_VMEM_BYTES, need
    return int(need)


def _rms_norm(x, gain):
    inv = lax.rsqrt(jnp.mean(x * x, axis=-1, keepdims=True) + NORM_EPS)
    return (x * inv) * gain


def _schedule_anchor(values):
    rows, cols = values.shape
    t = values[0:V7X_SUBLANES, :]
    for r0 in range(V7X_SUBLANES, rows, V7X_SUBLANES):
        t = jnp.maximum(t, values[r0:r0 + V7X_SUBLANES, :])
    acc = t[:, 0:V7X_LANES]
    for c0 in range(V7X_LANES, cols, V7X_LANES):
        acc = jnp.maximum(acc, t[:, c0:c0 + V7X_LANES])
    bits = pltpu.bitcast(acc, jnp.uint32)
    return pltpu.bitcast((bits >> 16) >> 16, _F32)


def _add_to_first_tile(array, tile):
    top = jnp.concatenate([array[0:V7X_SUBLANES, 0:V7X_LANES] + tile,
                           array[0:V7X_SUBLANES, V7X_LANES:]], axis=1)
    return jnp.concatenate([top, array[V7X_SUBLANES:, :]], axis=0)


def _resident(shape):
    zeros = (0,) * len(shape)
    return pl.BlockSpec(shape, lambda i: zeros, pipeline_mode=pl.Buffered(1))


_HBM = pl.BlockSpec(memory_space=pl.ANY)


def _next_first_sub_tile(tokens, tile, sub):
    last = tokens // sub - 1
    return pl.BlockSpec((sub, D_MODEL), lambda i: (jnp.minimum((i + 1) * (tile // sub), last), 0))


def _stage_shape(weight_shape):
    rows, cols = weight_shape
    bf16_rows = 2 * V7X_SUBLANES
    fits = [r for r in range(bf16_rows, WEIGHT_STAGE_BYTES // (cols * 4) + 1, bf16_rows)
            if rows % r == 0]
    return (WEIGHT_STAGE_SLOTS, max(fits), cols)


def _load_weights_bf16(jobs, stages, sem_ref):
    chunks, used = [], [0] * len(stages)
    for src, dst, si in jobs:
        slots, rows, _ = stages[si].shape
        for r0 in range(0, dst.shape[0], rows):
            chunks.append((src, dst, si, used[si] % slots, r0, rows))
            used[si] += 1

    def copy(chunk):
        src, _, si, slot, r0, rows = chunk
        return pltpu.make_async_copy(src.at[pl.ds(r0, rows), :], stages[si].at[slot],
                                     sem_ref.at[si, slot])

    ahead = min(stage.shape[0] for stage in stages) - 1
    for chunk in chunks[:ahead]:
        copy(chunk).start()
    for k, chunk in enumerate(chunks):
        if k + ahead < len(chunks):
            copy(chunks[k + ahead]).start()
        copy(chunk).wait()
        _, dst, si, slot, r0, rows = chunk
        dst[r0:r0 + rows, :] = stages[si][slot].astype(_BF16)


def _ffn_body(x_ref, xnext_ref, gain_ref, wg_hbm, wu_hbm, wd_hbm, fgain_ref, o_ref,
              h_ref, a_ref, wg_ref, wu_ref, wd_ref, stage_in_ref, stage_out_ref, sem_ref,
              *, layer, final_norm):
    subs = [slice(r0, r0 + FFN_SUB_TILE) for r0 in range(0, x_ref.shape[0], FFN_SUB_TILE)]

    def norm_into(rows, src):
        normed = _rms_norm(src, gain_ref[...])
        h_ref[rows, :] = normed.astype(_BF16)
        return normed

    @pl.when(pl.program_id(0) == 0)
    def _():
        _load_weights_bf16([(wg_hbm.at[layer], wg_ref, 0), (wu_hbm.at[layer], wu_ref, 0),
                            (wd_hbm.at[layer], wd_ref, 1)], [stage_in_ref, stage_out_ref], sem_ref)

    chunks = [(sum(FF_CHUNKS[:i]), sum(FF_CHUNKS[:i + 1])) for i in range(len(FF_CHUNKS))]
    out_blocks = list(range(0, D_MODEL, FFN_OUT_BLOCK))

    def up_item(si, ci):
        rows, (c0, c1) = subs[si], chunks[ci]
        if ci == 0 and si > 0:
            norm_into(rows, x_ref[rows, :])
        h = h_ref[rows, :]
        g = jnp.dot(h, wg_ref[:, c0:c1], preferred_element_type=_F32)
        u = jnp.dot(h, wu_ref[:, c0:c1], preferred_element_type=_F32)
        a_ref[rows, c0:c1] = (g / (1.0 + jnp.exp(-g)) * u).astype(_BF16)

    def first_item_of_next_step():
        norm_into(subs[0], xnext_ref[...])
        up_item(0, 0)

    done = {}

    def down_item(si, n0):
        rows, cols = subs[si], slice(n0, n0 + FFN_OUT_BLOCK)
        y = x_ref[rows, cols] + 0.5 * jnp.dot(a_ref[rows, :], wd_ref[:, cols],
                                              preferred_element_type=_F32)
        if not final_norm:
            o_ref[rows, cols] = y
            return
        done.setdefault(si, []).append(y)
        if n0 == out_blocks[-1]:
            o_ref[rows, :] = _rms_norm(jnp.concatenate(done.pop(si), axis=1), fgain_ref[...])

    @pl.when(pl.program_id(0) == 0)
    def _():
        norm_into(subs[0], x_ref[subs[0], :])
        up_item(0, 0)

    ups = [[functools.partial(up_item, si, ci) for ci in range(1 if si == 0 else 0, len(chunks))]
           for si in range(len(subs))]
    downs = [[functools.partial(down_item, si, n0) for n0 in out_blocks] for si in range(len(subs))]
    every = -(-len(chunks) // len(out_blocks))
    for si in range(len(subs)):
        for k, up in enumerate(ups[si]):
            if si > 0 and k % every == 0 and downs[si - 1]:
                downs[si - 1].pop(0)()
            up()
        if si > 0:
            for down in downs[si - 1]:
                down()
    first_item_of_next_step()
    for down in downs[-1]:
        down()


def _ffn(x, gain, wg, wu, wd, fgain, *, layer, final_norm):
    tokens = x.shape[0]
    tm = FFN_TOKEN_TILE
    assert tokens % tm == 0 and tm % FFN_SUB_TILE == 0 and sum(FF_CHUNKS) == D_FF
    stage_in, stage_out = _stage_shape((D_MODEL, D_FF)), _stage_shape((D_FF, D_MODEL))
    tile_bytes = (2 * tm + FFN_SUB_TILE) * D_MODEL * 4
    resident = (3 * D_MODEL * D_FF * 2 + 2 * D_MODEL * 4
                + 4 * (math.prod(stage_in) + math.prod(stage_out)))
    temps = (tm * (D_MODEL + D_FF) * 2 + 2 * 3 * FFN_SUB_TILE * max(FF_CHUNKS) * 4
             + 2 * FFN_SUB_TILE * D_MODEL * 4)
    row = pl.BlockSpec((tm, D_MODEL), lambda i: (i, 0))
    return pl.pallas_call(
        functools.partial(_ffn_body, layer=layer, final_norm=final_norm),
        grid=(tokens // tm,),
        in_specs=[row, _next_first_sub_tile(tokens, tm, FFN_SUB_TILE), _resident((1, D_MODEL)),
                  _HBM, _HBM, _HBM, _resident((1, D_MODEL))],
        out_specs=row,
        out_shape=jax.ShapeDtypeStruct(x.shape, _F32),
        scratch_shapes=[pltpu.VMEM((tm, D_MODEL), _BF16), pltpu.VMEM((tm, D_FF), _BF16),
                        pltpu.VMEM((D_MODEL, D_FF), _BF16), pltpu.VMEM((D_MODEL, D_FF), _BF16),
                        pltpu.VMEM((D_FF, D_MODEL), _BF16),
                        pltpu.VMEM(stage_in, _F32), pltpu.VMEM(stage_out, _F32),
                        pltpu.SemaphoreType.DMA((2, WEIGHT_STAGE_SLOTS))],
        compiler_params=pltpu.CompilerParams(
            dimension_semantics=("arbitrary",),
            vmem_limit_bytes=_vmem_limit(resident, tile_bytes, temps)),
        name="ffn_final" if final_norm else "ffn",
    )(x, x, gain, wg, wu, wd, fgain)


def _attn_bias_table():
    kj = np.arange(2 * BLOCK)[:, None]
    qi = np.arange(BLOCK)[None, :]
    dist = (qi + BLOCK - kj).astype(np.float32)
    in_window = (dist >= 0) & (dist < WINDOW)
    slopes = np.exp2(-8.0 * np.arange(1, N_Q_HEADS + 1, dtype=np.float32) / N_Q_HEADS)
    per_head = np.where(in_window[None], -slopes[:, None, None] * dist[None] * np.float32(LOG2E),
                        np.float32(MASK_VALUE))
    table = [np.concatenate([per_head[GQA_GROUP * g + half], per_head[GQA_GROUP * g + 2 + half]], axis=1)
             for g in range(N_KV_HEADS) for half in range(2)]
    return np.stack(table).astype(np.float32)


def _place_heads(t):
    lo = lax.broadcasted_iota(jnp.int32, t.shape, 1) < HEAD_DIM
    r = pltpu.roll(t, HEAD_DIM, axis=1)
    zero = jnp.zeros_like(t)
    placed = (jnp.where(lo, t, zero), jnp.where(lo, r, zero),
              jnp.where(lo, zero, r), jnp.where(lo, zero, t))
    return [p.astype(_BF16) for p in placed]


def _mixer_body(sink_ref, x_ref, gain_ref, win_hbm, convw_ref, wout_hbm, bias_ref,
                o_ref, kp_ref, vt_ref, u_ref, mix_ref, s_ref, h_ref,
                win_ref, wout_ref, stage_in_ref, stage_out_ref, sem_ref, *, layer, tiles_per_seq):
    tm = x_ref.shape[0]
    first = pl.program_id(0) % tiles_per_seq == 0

    @pl.when(pl.program_id(0) == 0)
    def _():
        _load_weights_bf16([(win_hbm.at[layer], win_ref, 0), (wout_hbm.at[layer], wout_ref, 1)],
                           [stage_in_ref, stage_out_ref], sem_ref)

    @pl.when(first)
    def _():
        kp_ref[:, 0:BLOCK, :] = jnp.zeros((4, BLOCK, V7X_LANES), _BF16)
        vt_ref[:, 0:BLOCK] = jnp.zeros((KV_WIDTH, BLOCK), _BF16)
        u_ref[0:V7X_SUBLANES, :] = jnp.zeros((V7X_SUBLANES, CONV_WIDTH), _F32)

    sub = MIXER_SUB_TILE
    n_sub = tm // sub
    val = {}

    def rows(s, lo=0, hi=sub):
        return slice(s * sub + lo, s * sub + hi)

    def proj(s, c0, c1):
        return jnp.dot(h_ref[rows(s), :], win_ref[:, c0:c1], preferred_element_type=_F32)

    def conv_in(s):
        h_ref[rows(s), :] = _rms_norm(x_ref[rows(s), :], gain_ref[...]).astype(_BF16)
        val["cc", s] = proj(s, CB_END, CC_END)

    def conv_taps(s):
        u = val.pop(("cc", s)) * proj(s, CC_END, CH_END)
        base = V7X_SUBLANES + s * sub
        u_ref[base:base + sub, :] = u
        cw = convw_ref[...]
        val["y", s] = (cw[0:1, :] * u_ref[base - 2:base - 2 + sub, :]
                       + cw[1:2, :] * u_ref[base - 1:base - 1 + sub, :]
                       + cw[2:3, :] * u)

    def step_q(s):
        val["q", s] = (proj(s, 0, Q_END) * (LOG2E / math.sqrt(HEAD_DIM))).astype(_BF16)

    def step_kv(s):
        kv = proj(s, Q_END, KV_END)
        band = slice(BLOCK + s * sub, BLOCK + (s + 1) * sub)
        for idx, p in enumerate(_place_heads(kv[:, :KV_WIDTH])):
            kp_ref[idx, band, :] = p
        vt_ref[:, band] = kv[:, KV_WIDTH:].T.astype(_BF16)

    def conv_gate(s):
        mix_ref[rows(s), ATTN_WIDTH:] = (proj(s, KV_END, CB_END) * val.pop(("y", s))).astype(_BF16)

    projection_steps = (conv_in, conv_taps, step_q, step_kv, conv_gate)

    def out_proj(s, lo, hi):
        r = rows(s, lo, hi)
        o_ref[r, :] = x_ref[r, :] + jnp.dot(mix_ref[r, :], wout_ref[...],
                                            preferred_element_type=_F32)

    prev_mask = jnp.where(first, MASK_VALUE, 0.0).astype(_F32)
    lane = lax.broadcasted_iota(jnp.int32, (1, 2 * BLOCK), 1)
    nt = (((1,), (1,)), ((), ()))
    chains = [(j, g, half) for j in range(sub // BLOCK) for g in range(N_KV_HEADS)
              for half in range(2)]

    def scores(s, j, g, half):
        q = val["q", s]
        qrows = slice(j * BLOCK, (j + 1) * BLOCK)
        qg = jnp.concatenate([q[qrows, (2 * g) * V7X_LANES:(2 * g + 1) * V7X_LANES],
                              q[qrows, (2 * g + 1) * V7X_LANES:(2 * g + 2) * V7X_LANES]], axis=0)
        band = slice(s * sub + j * BLOCK, s * sub + (j + 2) * BLOCK)
        return lax.dot_general(kp_ref[2 * half + g, band, :], qg, nt,
                               preferred_element_type=_F32)

    def score_phase(s):
        stats = []
        for c, (j, g, half) in enumerate(chains):
            sc = scores(s, j, g, half) + bias_ref[2 * g + half]
            if s == 0 and j == 0:
                sc = jnp.concatenate([sc[:BLOCK] + prev_mask, sc[BLOCK:]], axis=0)
            s_ref[c] = sc
            sink = LOG2E * jnp.where(lane < BLOCK, sink_ref[GQA_GROUP * g + half],
                                     sink_ref[GQA_GROUP * g + 2 + half])
            stats.append((jnp.maximum(jnp.max(sc, axis=0, keepdims=True), sink), sink))
        val.pop(("q", s))
        return stats

    def value_phase(s, stats, fillers):
        outs = {}
        for c, (j, g, half) in enumerate(chains):
            if c in fillers:
                fillers[c]()
            m, sink = stats[c]
            p = jnp.exp2(s_ref[c] - m)
            denom = jnp.sum(p, axis=0, keepdims=True) + jnp.exp2(sink - m)
            band = slice(s * sub + j * BLOCK, s * sub + (j + 2) * BLOCK)
            o = jnp.dot(vt_ref[:, band], p.astype(_BF16), preferred_element_type=_F32)
            outs[half] = o[HEAD_DIM * g:HEAD_DIM * (g + 1), :] * (1.0 / denom)
            if half == 1:
                for pr in range(2):
                    qcols = slice(pr * BLOCK, (pr + 1) * BLOCK)
                    at = jnp.concatenate([outs[0][:, qcols], outs[1][:, qcols]], axis=0)
                    mix_ref[rows(s, j * BLOCK, (j + 1) * BLOCK),
                            (2 * g + pr) * V7X_LANES:(2 * g + pr + 1) * V7X_LANES] = (
                                at.T.astype(_BF16))

    def spread(thunks, first, last):
        return {first + (i * (last - first)) // len(thunks): t for i, t in enumerate(thunks)}

    for step in projection_steps:
        step(0)
    pending = []
    for s in range(n_sub):
        stats = score_phase(s)
        own_half_done = len(chains) // 2 + 1
        if s + 1 < n_sub:
            fillers = spread(pending + [functools.partial(step, s + 1) for step in projection_steps],
                             0, len(chains))
            pending = [functools.partial(out_proj, s, 0, sub // 2),
                       functools.partial(out_proj, s, sub // 2, sub)]
        else:
            fillers = spread(pending, 0, own_half_done) if pending else {}
            fillers[own_half_done] = functools.partial(out_proj, s, 0, sub // 2)
            pending = [functools.partial(out_proj, s, sub // 2, sub)]
        value_phase(s, stats, fillers)
    for thunk in pending:
        thunk()

    u_ref[0:V7X_SUBLANES, :] = u_ref[tm:tm + V7X_SUBLANES, :]
    kp_ref[:, 0:BLOCK, :] = kp_ref[:, tm:tm + BLOCK, :]
    vt_ref[:, 0:BLOCK] = vt_ref[:, tm:tm + BLOCK]


def _mixer(x, gain, w_in, conv_w, sink, w_out, seq_len, *, layer):
    tokens = x.shape[0]
    tm = TOKEN_TILE
    sub = MIXER_SUB_TILE
    assert seq_len % tm == 0 and tm % sub == 0 and sub % BLOCK == 0
    bias = jnp.asarray(_attn_bias_table())
    n_chains = (sub // BLOCK) * N_KV_HEADS * 2
    stage_in, stage_out = _stage_shape((D_MODEL, CH_END)), _stage_shape((MIX_WIDTH, D_MODEL))
    tile_bytes = 2 * tm * D_MODEL * 4
    resident = (D_MODEL * CH_END * 2 + MIX_WIDTH * D_MODEL * 2 + bias.size * 4
                + (D_MODEL + CONV_K * CONV_WIDTH) * 4
                + 4 * (math.prod(stage_in) + math.prod(stage_out)))
    scratch = ((4 + 1) * (BLOCK + tm) * V7X_LANES * 2 + (V7X_SUBLANES + tm) * CONV_WIDTH * 4
               + tm * MIX_WIDTH * 2 + n_chains * 4 * BLOCK * BLOCK * 4 + tm * D_MODEL * 2)
    temps = sub * CH_END * 4 + 8 * 1024 * 1024
    row = pl.BlockSpec((tm, D_MODEL), lambda i: (i, 0))
    return pl.pallas_call(
        functools.partial(_mixer_body, layer=layer, tiles_per_seq=seq_len // tm),
        grid=(tokens // tm,),
        in_specs=[pl.BlockSpec(memory_space=pltpu.SMEM), row, _resident((1, D_MODEL)),
                  _HBM, _resident((CONV_K, CONV_WIDTH)), _HBM, _resident(bias.shape)],
        out_specs=row,
        out_shape=jax.ShapeDtypeStruct(x.shape, _F32),
        scratch_shapes=[pltpu.VMEM((4, BLOCK + tm, V7X_LANES), _BF16),
                        pltpu.VMEM((KV_WIDTH, BLOCK + tm), _BF16),
                        pltpu.VMEM((V7X_SUBLANES + tm, CONV_WIDTH), _F32),
                        pltpu.VMEM((tm, MIX_WIDTH), _BF16),
                        pltpu.VMEM((n_chains, 2 * BLOCK, 2 * BLOCK), _F32),
                        pltpu.VMEM((tm, D_MODEL), _BF16),
                        pltpu.VMEM((D_MODEL, CH_END), _BF16), pltpu.VMEM((MIX_WIDTH, D_MODEL), _BF16),
                        pltpu.VMEM(stage_in, _F32), pltpu.VMEM(stage_out, _F32),
                        pltpu.SemaphoreType.DMA((2, WEIGHT_STAGE_SLOTS))],
        compiler_params=pltpu.CompilerParams(
            dimension_semantics=("arbitrary",),
            vmem_limit_bytes=_vmem_limit(resident + scratch, tile_bytes, temps)),
        name="mixer",
    )(sink, x, gain, w_in, conv_w, w_out, bias)


def kernel(x, ffn1_norm, ffn1_wg, ffn1_wu, ffn1_wd, mix_norm, w_in, conv_w, attn_sink, w_out,
           ffn2_norm, ffn2_wg, ffn2_wu, ffn2_wd, final_norm):
    bsz, seq, d = x.shape
    depth = w_in.shape[0]
    xt = x.reshape(bsz * seq, d)
    fgain = final_norm.reshape(1, d)
    for l in range(depth):
        xt = _ffn(xt, ffn1_norm[l].reshape(1, d), ffn1_wg, ffn1_wu, ffn1_wd, fgain,
                  layer=l, final_norm=False)
        xt = _mixer(xt, mix_norm[l].reshape(1, d), w_in, conv_w[l], attn_sink[l], w_out, seq,
                    layer=l)
        xt = _ffn(xt, ffn2_norm[l].reshape(1, d), ffn2_wg, ffn2_wu, ffn2_wd, fgain,
                  layer=l, final_norm=(l == depth - 1))
    return xt.reshape(bsz, seq, d)
```

```python
import functools
import math

import numpy as np
import jax
import jax.numpy as jnp
from jax import lax
from jax.experimental import pallas as pl
from jax.experimental.pallas import tpu as pltpu

D_MODEL = 1024
HEAD_DIM = 64
N_Q_HEADS = 8
N_KV_HEADS = 2
GQA_GROUP = N_Q_HEADS // N_KV_HEADS
ATTN_WIDTH = N_Q_HEADS * HEAD_DIM
KV_WIDTH = N_KV_HEADS * HEAD_DIM
CONV_WIDTH = 512
MIX_WIDTH = ATTN_WIDTH + CONV_WIDTH
CONV_K = 3
WINDOW = 128
BLOCK = 128
D_FF = 2816
NORM_EPS = 1e-6

Q_END = ATTN_WIDTH
KV_END = Q_END + 2 * KV_WIDTH
CB_END = KV_END + CONV_WIDTH
CC_END = CB_END + CONV_WIDTH
CH_END = CC_END + CONV_WIDTH

V7X_LANES = 128
V7X_SUBLANES = 8
V7X_MXU_DIM = 256
V7X_VMEM_BYTES = 64 * 1024 * 1024

TOKEN_TILE = 1024
MIXER_SUB_TILE = 512
VT_ROWS = KV_WIDTH + 2 * V7X_SUBLANES
FFN_TOKEN_TILE = 1024
FFN_SUB_TILE = 512
FFN_OUT_BLOCK = V7X_MXU_DIM
FF_CHUNKS = (V7X_MXU_DIM,) * (D_FF // V7X_MXU_DIM)
WEIGHT_STAGE_BYTES = 1536 * 1024
WEIGHT_STAGE_SLOTS = 3
MASK_VALUE = -1e30
LOG2E = math.log2(math.e)

_F32 = jnp.float32
_BF16 = jnp.bfloat16


def _vmem_limit(resident_bytes, tile_bytes, temp_bytes):
    need = resident_bytes + 2 * tile_bytes + temp_bytes
    assert need <= V7X_VMEM_BYTES, need
    return int(need)


def _rms_norm(x, gain):
    inv = lax.rsqrt(jnp.mean(x * x, axis=-1, keepdims=True) + NORM_EPS)
    return (x * inv) * gain


def _schedule_anchor(values):
    rows, cols = values.shape
    t = values[0:V7X_SUBLANES, :]
    for r0 in range(V7X_SUBLANES, rows, V7X_SUBLANES):
        t = jnp.maximum(t, values[r0:r0 + V7X_SUBLANES, :])
    acc = t[:, 0:V7X_LANES]
    for c0 in range(V7X_LANES, cols, V7X_LANES):
        acc = jnp.maximum(acc, t[:, c0:c0 + V7X_LANES])
    bits = pltpu.bitcast(acc, jnp.uint32)
    return pltpu.bitcast((bits >> 16) >> 16, _F32)


def _add_to_first_tile(array, tile):
    top = jnp.concatenate([array[0:V7X_SUBLANES, 0:V7X_LANES] + tile,
                           array[0:V7X_SUBLANES, V7X_LANES:]], axis=1)
    return jnp.concatenate([top, array[V7X_SUBLANES:, :]], axis=0)


def _resident(shape):
    zeros = (0,) * len(shape)
    return pl.BlockSpec(shape, lambda i: zeros, pipeline_mode=pl.Buffered(1))


_HBM = pl.BlockSpec(memory_space=pl.ANY)


def _next_first_sub_tile(tokens, tile, sub):
    last = tokens // sub - 1
    return pl.BlockSpec((sub, D_MODEL), lambda i: (jnp.minimum((i + 1) * (tile // sub), last), 0))


def _stage_shape(weight_shape):
    rows, cols = weight_shape
    bf16_rows = 2 * V7X_SUBLANES
    fits = [r for r in range(bf16_rows, WEIGHT_STAGE_BYTES // (cols * 4) + 1, bf16_rows)
            if rows % r == 0]
    return (WEIGHT_STAGE_SLOTS, max(fits), cols)


def _load_weights_bf16(jobs, stages, sem_ref):
    chunks, used = [], [0] * len(stages)
    for src, dst, si in jobs:
        slots, rows, _ = stages[si].shape
        for r0 in range(0, dst.shape[0], rows):
            chunks.append((src, dst, si, used[si] % slots, r0, rows))
            used[si] += 1

    def copy(chunk):
        src, _, si, slot, r0, rows = chunk
        return pltpu.make_async_copy(src.at[pl.ds(r0, rows), :], stages[si].at[slot],
                                     sem_ref.at[si, slot])

    ahead = min(stage.shape[0] for stage in stages) - 1
    for chunk in chunks[:ahead]:
        copy(chunk).start()
    for k, chunk in enumerate(chunks):
        if k + ahead < len(chunks):
            copy(chunks[k + ahead]).start()
        copy(chunk).wait()
        _, dst, si, slot, r0, rows = chunk
        dst[r0:r0 + rows, :] = stages[si][slot].astype(_BF16)


def _ffn_body(x_ref, xnext_ref, gain_ref, wg_hbm, wu_hbm, wd_hbm, fgain_ref, o_ref,
              h_ref, a_ref, wg_ref, wu_ref, wd_ref, stage_in_ref, stage_out_ref, sem_ref,
              *, layer, final_norm):
    subs = [slice(r0, r0 + FFN_SUB_TILE) for r0 in range(0, x_ref.shape[0], FFN_SUB_TILE)]

    def norm_into(rows, src):
        normed = _rms_norm(src, gain_ref[...])
        h_ref[rows, :] = normed.astype(_BF16)
        return normed

    @pl.when(pl.program_id(0) == 0)
    def _():
        _load_weights_bf16([(wg_hbm.at[layer], wg_ref, 0), (wu_hbm.at[layer], wu_ref, 0),
                            (wd_hbm.at[layer], wd_ref, 1)], [stage_in_ref, stage_out_ref], sem_ref)

    chunks = [(sum(FF_CHUNKS[:i]), sum(FF_CHUNKS[:i + 1])) for i in range(len(FF_CHUNKS))]
    out_blocks = list(range(0, D_MODEL, FFN_OUT_BLOCK))

    def up_item(si, ci):
        rows, (c0, c1) = subs[si], chunks[ci]
        if ci == 0 and si > 0:
            norm_into(rows, x_ref[rows, :])
        h = h_ref[rows, :]
        g = jnp.dot(h, wg_ref[:, c0:c1], preferred_element_type=_F32)
        u = jnp.dot(h, wu_ref[:, c0:c1], preferred_element_type=_F32)
        a_ref[rows, c0:c1] = (g / (1.0 + jnp.exp(-g)) * u).astype(_BF16)

    def first_item_of_next_step():
        norm_into(subs[0], xnext_ref[...])
        up_item(0, 0)

    done = {}

    def down_item(si, n0):
        rows, cols = subs[si], slice(n0, n0 + FFN_OUT_BLOCK)
        y = x_ref[rows, cols] + 0.5 * jnp.dot(a_ref[rows, :], wd_ref[:, cols],
                                              preferred_element_type=_F32)
        if not final_norm:
            o_ref[rows, cols] = y
            return
        done.setdefault(si, []).append(y)
        if n0 == out_blocks[-1]:
            o_ref[rows, :] = _rms_norm(jnp.concatenate(done.pop(si), axis=1), fgain_ref[...])

    @pl.when(pl.program_id(0) == 0)
    def _():
        norm_into(subs[0], x_ref[subs[0], :])
        up_item(0, 0)

    ups = [[functools.partial(up_item, si, ci) for ci in range(1 if si == 0 else 0, len(chunks))]
           for si in range(len(subs))]
    downs = [[functools.partial(down_item, si, n0) for n0 in out_blocks] for si in range(len(subs))]
    every = -(-len(chunks) // len(out_blocks))
    for si in range(len(subs)):
        for k, up in enumerate(ups[si]):
            if si > 0 and k % every == 0 and downs[si - 1]:
                downs[si - 1].pop(0)()
            up()
        if si > 0:
            for down in downs[si - 1]:
                down()
    first_item_of_next_step()
    for down in downs[-1]:
        down()


def _ffn(x, gain, wg, wu, wd, fgain, *, layer, final_norm):
    tokens = x.shape[0]
    tm = FFN_TOKEN_TILE
    assert tokens % tm == 0 and tm % FFN_SUB_TILE == 0 and sum(FF_CHUNKS) == D_FF
    stage_in, stage_out = _stage_shape((D_MODEL, D_FF)), _stage_shape((D_FF, D_MODEL))
    tile_bytes = (2 * tm + FFN_SUB_TILE) * D_MODEL * 4
    resident = (3 * D_MODEL * D_FF * 2 + 2 * D_MODEL * 4
                + 4 * (math.prod(stage_in) + math.prod(stage_out)))
    temps = (tm * (D_MODEL + D_FF) * 2 + 2 * 3 * FFN_SUB_TILE * max(FF_CHUNKS) * 4
             + 2 * FFN_SUB_TILE * D_MODEL * 4)
    row = pl.BlockSpec((tm, D_MODEL), lambda i: (i, 0))
    return pl.pallas_call(
        functools.partial(_ffn_body, layer=layer, final_norm=final_norm),
        grid=(tokens // tm,),
        in_specs=[row, _next_first_sub_tile(tokens, tm, FFN_SUB_TILE), _resident((1, D_MODEL)),
                  _HBM, _HBM, _HBM, _resident((1, D_MODEL))],
        out_specs=row,
        out_shape=jax.ShapeDtypeStruct(x.shape, _F32),
        scratch_shapes=[pltpu.VMEM((tm, D_MODEL), _BF16), pltpu.VMEM((tm, D_FF), _BF16),
                        pltpu.VMEM((D_MODEL, D_FF), _BF16), pltpu.VMEM((D_MODEL, D_FF), _BF16),
                        pltpu.VMEM((D_FF, D_MODEL), _BF16),
                        pltpu.VMEM(stage_in, _F32), pltpu.VMEM(stage_out, _F32),
                        pltpu.SemaphoreType.DMA((2, WEIGHT_STAGE_SLOTS))],
        compiler_params=pltpu.CompilerParams(
            dimension_semantics=("arbitrary",),
            vmem_limit_bytes=_vmem_limit(resident, tile_bytes, temps)),
        name="ffn_final" if final_norm else "ffn",
    )(x, x, gain, wg, wu, wd, fgain)


def _attn_bias_table():
    kj = np.arange(2 * BLOCK)[:, None]
    qi = np.arange(BLOCK)[None, :]
    dist = (qi + BLOCK - kj).astype(np.float32)
    in_window = (dist >= 0) & (dist < WINDOW)
    slopes = np.exp2(-8.0 * np.arange(1, N_Q_HEADS + 1, dtype=np.float32) / N_Q_HEADS)
    per_head = np.where(in_window[None], -slopes[:, None, None] * dist[None] * np.float32(LOG2E),
                        np.float32(MASK_VALUE))
    table = [np.concatenate([per_head[GQA_GROUP * g + half], per_head[GQA_GROUP * g + 2 + half]], axis=1)
             for g in range(N_KV_HEADS) for half in range(2)]
    return np.stack(table).astype(np.float32)


def _place_heads(t):
    lo = lax.broadcasted_iota(jnp.int32, t.shape, 1) < HEAD_DIM
    r = pltpu.roll(t, HEAD_DIM, axis=1)
    zero = jnp.zeros_like(t)
    placed = (jnp.where(lo, t, zero), jnp.where(lo, r, zero),
              jnp.where(lo, zero, r), jnp.where(lo, zero, t))
    return [p.astype(_BF16) for p in placed]


def _mixer_body(sink_ref, x_ref, xnext_ref, gain_ref, win_hbm, convw_ref, wout_hbm, bias_ref,
                o_ref, kp_ref, vt_ref, u_ref, mix_ref, s_ref, h_ref, cc_ref,
                win_ref, wout_ref, stage_in_ref, stage_out_ref, sem_ref, *, layer, tiles_per_seq):
    tm = x_ref.shape[0]
    first = pl.program_id(0) % tiles_per_seq == 0

    @pl.when(pl.program_id(0) == 0)
    def _():
        _load_weights_bf16([(win_hbm.at[layer], win_ref, 0), (wout_hbm.at[layer], wout_ref, 1)],
                           [stage_in_ref, stage_out_ref], sem_ref)
        pad_rows = lax.broadcasted_iota(jnp.int32, (VT_ROWS - KV_WIDTH, vt_ref.shape[1]), 0)
        vt_ref[KV_WIDTH:, :] = jnp.where(pad_rows == 0, 1.0, 0.0).astype(_BF16)

    @pl.when(first)
    def _():
        kp_ref[:, 0:BLOCK, :] = jnp.zeros((4, BLOCK, V7X_LANES), _BF16)
        vt_ref[0:KV_WIDTH, 0:BLOCK] = jnp.zeros((KV_WIDTH, BLOCK), _BF16)
        u_ref[0:V7X_SUBLANES, :] = jnp.zeros((V7X_SUBLANES, CONV_WIDTH), _F32)

    sub = MIXER_SUB_TILE
    n_sub = tm // sub
    val = {}

    def rows(s, lo=0, hi=sub):
        return slice(s * sub + lo, s * sub + hi)

    def proj(s, c0, c1):
        return jnp.dot(h_ref[rows(s), :], win_ref[:, c0:c1], preferred_element_type=_F32)

    def conv_in(s, src=None):
        src = x_ref[rows(s), :] if src is None else src
        h_ref[rows(s), :] = _rms_norm(src, gain_ref[...]).astype(_BF16)
        if s == 0:
            cc_ref[...] = proj(s, CB_END, CC_END)
        else:
            val["cc", s] = proj(s, CB_END, CC_END)

    def conv_taps(s):
        u = (cc_ref[...] if s == 0 else val.pop(("cc", s))) * proj(s, CC_END, CH_END)
        base = V7X_SUBLANES + s * sub
        u_ref[base:base + sub, :] = u
        prev = u_ref[base - V7X_SUBLANES:base, :]
        row = lax.broadcasted_iota(jnp.int32, prev.shape, 0)

        def shifted(k):
            rolled = pltpu.roll(u, k, axis=0)
            head = jnp.where(row < k, pltpu.roll(prev, k, axis=0), rolled[0:V7X_SUBLANES, :])
            return jnp.concatenate([head, rolled[V7X_SUBLANES:, :]], axis=0)

        cw = convw_ref[...]
        val["y", s] = cw[0:1, :] * shifted(2) + cw[1:2, :] * shifted(1) + cw[2:3, :] * u

    def step_q(s):
        val["q", s] = (proj(s, 0, Q_END) * (LOG2E / math.sqrt(HEAD_DIM))).astype(_BF16)

    def step_kv(s):
        kv = proj(s, Q_END, KV_END)
        band = slice(BLOCK + s * sub, BLOCK + (s + 1) * sub)
        for idx, p in enumerate(_place_heads(kv[:, :KV_WIDTH])):
            kp_ref[idx, band, :] = p
        vt_ref[0:KV_WIDTH, band] = kv[:, KV_WIDTH:].T.astype(_BF16)

    def conv_gate(s):
        mix_ref[rows(s), ATTN_WIDTH:] = (proj(s, KV_END, CB_END) * val.pop(("y", s))).astype(_BF16)

    projection_steps = (conv_in, conv_taps, step_q, step_kv, conv_gate)

    def out_proj(s, lo, hi):
        r = rows(s, lo, hi)
        o_ref[r, :] = x_ref[r, :] + jnp.dot(mix_ref[r, :], wout_ref[...],
                                            preferred_element_type=_F32)

    prev_mask = jnp.where(first, MASK_VALUE, 0.0).astype(_F32)
    lane = lax.broadcasted_iota(jnp.int32, (1, 2 * BLOCK), 1)
    nt = (((1,), (1,)), ((), ()))
    chains = [(j, g, half) for j in range(sub // BLOCK) for g in range(N_KV_HEADS)
              for half in range(2)]

    def scores(s, j, g, half):
        q = val["q", s]
        qrows = slice(j * BLOCK, (j + 1) * BLOCK)
        qg = jnp.concatenate([q[qrows, (2 * g) * V7X_LANES:(2 * g + 1) * V7X_LANES],
                              q[qrows, (2 * g + 1) * V7X_LANES:(2 * g + 2) * V7X_LANES]], axis=0)
        band = slice(s * sub + j * BLOCK, s * sub + (j + 2) * BLOCK)
        return lax.dot_general(kp_ref[2 * half + g, band, :], qg, nt,
                               preferred_element_type=_F32)

    def score_phase(s):
        stats = []
        for c, (j, g, half) in enumerate(chains):
            sc = scores(s, j, g, half) + bias_ref[2 * g + half]
            if s == 0 and j == 0:
                sc = jnp.concatenate([sc[:BLOCK] + prev_mask, sc[BLOCK:]], axis=0)
            s_ref[c] = sc
            sink = LOG2E * jnp.where(lane < BLOCK, sink_ref[GQA_GROUP * g + half],
                                     sink_ref[GQA_GROUP * g + 2 + half])
            stats.append((jnp.maximum(jnp.max(sc, axis=0, keepdims=True), sink), sink))
        val.pop(("q", s))
        return stats

    def value_phase(s, stats, fillers):
        outs = {}
        for c, (j, g, half) in enumerate(chains):
            if c in fillers:
                fillers[c]()
            m, sink = stats[c]
            p = jnp.exp2(s_ref[c] - m).astype(_BF16)
            band = slice(s * sub + j * BLOCK, s * sub + (j + 2) * BLOCK)
            o = jnp.dot(vt_ref[:, band], p, preferred_element_type=_F32)
            denom = o[KV_WIDTH:KV_WIDTH + 1, :] + jnp.exp2(sink - m)
            outs[half] = o[HEAD_DIM * g:HEAD_DIM * (g + 1), :] * (1.0 / denom)
            if half == 1:
                for pr in range(2):
                    qcols = slice(pr * BLOCK, (pr + 1) * BLOCK)
                    at = jnp.concatenate([outs[0][:, qcols], outs[1][:, qcols]], axis=0)
                    mix_ref[rows(s, j * BLOCK, (j + 1) * BLOCK),
                            (2 * g + pr) * V7X_LANES:(2 * g + pr + 1) * V7X_LANES] = (
                                at.T.astype(_BF16))

    def spread(thunks, first, last):
        return {first + (i * (last - first)) // len(thunks): t for i, t in enumerate(thunks)}

    @pl.when(pl.program_id(0) == 0)
    def _():
        conv_in(0)

    for step in projection_steps[1:]:
        step(0)
    pending = []
    for s in range(n_sub):
        stats = score_phase(s)
        own_half_done = len(chains) // 2 + 1
        if s + 1 < n_sub:
            fillers = spread(pending + [functools.partial(step, s + 1) for step in projection_steps],
                             0, len(chains))
            pending = [functools.partial(out_proj, s, 0, sub // 2),
                       functools.partial(out_proj, s, sub // 2, sub)]
        else:
            fillers = spread(pending, 0, own_half_done) if pending else {}
            fillers[own_half_done] = functools.partial(out_proj, s, 0, sub // 2)
            pending = [functools.partial(out_proj, s, sub // 2, sub)]
        value_phase(s, stats, fillers)
    conv_in(0, xnext_ref[...])
    for thunk in pending:
        thunk()

    u_ref[0:V7X_SUBLANES, :] = u_ref[tm:tm + V7X_SUBLANES, :]
    kp_ref[:, 0:BLOCK, :] = kp_ref[:, tm:tm + BLOCK, :]
    vt_ref[0:KV_WIDTH, 0:BLOCK] = vt_ref[0:KV_WIDTH, tm:tm + BLOCK]


def _mixer(x, gain, w_in, conv_w, sink, w_out, seq_len, *, layer):
    tokens = x.shape[0]
    tm = TOKEN_TILE
    sub = MIXER_SUB_TILE
    assert seq_len % tm == 0 and tm % sub == 0 and sub % BLOCK == 0
    bias = jnp.asarray(_attn_bias_table())
    n_chains = (sub // BLOCK) * N_KV_HEADS * 2
    stage_in, stage_out = _stage_shape((D_MODEL, CH_END)), _stage_shape((MIX_WIDTH, D_MODEL))
    tile_bytes = (2 * tm + sub) * D_MODEL * 4
    resident = (D_MODEL * CH_END * 2 + MIX_WIDTH * D_MODEL * 2 + bias.size * 4
                + (D_MODEL + CONV_K * CONV_WIDTH) * 4
                + 4 * (math.prod(stage_in) + math.prod(stage_out)))
    scratch = ((4 * V7X_LANES + VT_ROWS) * (BLOCK + tm) * 2 + (V7X_SUBLANES + tm) * CONV_WIDTH * 4
               + tm * MIX_WIDTH * 2 + n_chains * 4 * BLOCK * BLOCK * 4 + tm * D_MODEL * 2
               + sub * CONV_WIDTH * 4)
    temps = sub * CH_END * 4 + 8 * 1024 * 1024
    row = pl.BlockSpec((tm, D_MODEL), lambda i: (i, 0))
    return pl.pallas_call(
        functools.partial(_mixer_body, layer=layer, tiles_per_seq=seq_len // tm),
        grid=(tokens // tm,),
        in_specs=[pl.BlockSpec(memory_space=pltpu.SMEM), row, _next_first_sub_tile(tokens, tm, sub),
                  _resident((1, D_MODEL)), _HBM, _resident((CONV_K, CONV_WIDTH)), _HBM,
                  _resident(bias.shape)],
        out_specs=row,
        out_shape=jax.ShapeDtypeStruct(x.shape, _F32),
        scratch_shapes=[pltpu.VMEM((4, BLOCK + tm, V7X_LANES), _BF16),
                        pltpu.VMEM((VT_ROWS, BLOCK + tm), _BF16),
                        pltpu.VMEM((V7X_SUBLANES + tm, CONV_WIDTH), _F32),
                        pltpu.VMEM((tm, MIX_WIDTH), _BF16),
                        pltpu.VMEM((n_chains, 2 * BLOCK, 2 * BLOCK), _F32),
                        pltpu.VMEM((tm, D_MODEL), _BF16),
                        pltpu.VMEM((sub, CONV_WIDTH), _F32),
                        pltpu.VMEM((D_MODEL, CH_END), _BF16), pltpu.VMEM((MIX_WIDTH, D_MODEL), _BF16),
                        pltpu.VMEM(stage_in, _F32), pltpu.VMEM(stage_out, _F32),
                        pltpu.SemaphoreType.DMA((2, WEIGHT_STAGE_SLOTS))],
        compiler_params=pltpu.CompilerParams(
            dimension_semantics=("arbitrary",),
            vmem_limit_bytes=_vmem_limit(resident + scratch, tile_bytes, temps)),
        name="mixer",
    )(sink, x, x, gain, w_in, conv_w, w_out, bias)


def kernel(x, ffn1_norm, ffn1_wg, ffn1_wu, ffn1_wd, mix_norm, w_in, conv_w, attn_sink, w_out,
           ffn2_norm, ffn2_wg, ffn2_wu, ffn2_wd, final_norm):
    bsz, seq, d = x.shape
    depth = w_in.shape[0]
    xt = x.reshape(bsz * seq, d)
    fgain = final_norm.reshape(1, d)
    for l in range(depth):
        xt = _ffn(xt, ffn1_norm[l].reshape(1, d), ffn1_wg, ffn1_wu, ffn1_wd, fgain,
                  layer=l, final_norm=False)
        xt = _mixer(xt, mix_norm[l].reshape(1, d), w_in, conv_w[l], attn_sink[l], w_out, seq,
                    layer=l)
        xt = _ffn(xt, ffn2_norm[l].reshape(1, d), ffn2_wg, ffn2_wu, ffn2_wd, fgain,
                  layer=l, final_norm=(l == depth - 1))
    return xt.reshape(bsz, seq, d)
```

```python
import functools
import math

import numpy as np
import jax
import jax.numpy as jnp
from jax import lax
from jax.experimental import pallas as pl
from jax.experimental.pallas import tpu as pltpu

D_MODEL = 1024
HEAD_DIM = 64
N_Q_HEADS = 8
N_KV_HEADS = 2
GQA_GROUP = N_Q_HEADS // N_KV_HEADS
ATTN_WIDTH = N_Q_HEADS * HEAD_DIM
KV_WIDTH = N_KV_HEADS * HEAD_DIM
CONV_WIDTH = 512
MIX_WIDTH = ATTN_WIDTH + CONV_WIDTH
CONV_K = 3
WINDOW = 128
BLOCK = 128
D_FF = 2816
NORM_EPS = 1e-6

Q_END = ATTN_WIDTH
KV_END = Q_END + 2 * KV_WIDTH
CB_END = KV_END + CONV_WIDTH
CC_END = CB_END + CONV_WIDTH
CH_END = CC_END + CONV_WIDTH

V7X_LANES = 128
V7X_SUBLANES = 8
V7X_MXU_DIM = 256
V7X_VMEM_BYTES = 64 * 1024 * 1024

TOKEN_TILE = 1024
MIXER_SUB_TILE = 512
VT_ROWS = KV_WIDTH + 2 * V7X_SUBLANES
FFN_TOKEN_TILE = 1024
FFN_SUB_TILE = 512
FFN_OUT_BLOCK = V7X_MXU_DIM
FF_CHUNKS = (V7X_MXU_DIM,) * (D_FF // V7X_MXU_DIM)
WEIGHT_STAGE_BYTES = 1536 * 1024
WEIGHT_STAGE_SLOTS = 3
MASK_VALUE = -1e30
LOG2E = math.log2(math.e)

_F32 = jnp.float32
_BF16 = jnp.bfloat16


def _vmem_limit(resident_bytes, tile_bytes, temp_bytes):
    need = resident_bytes + 2 * tile_bytes + temp_bytes
    assert need <= V7X_VMEM_BYTES, need
    return int(need)


def _rms_norm(x, gain):
    inv = lax.rsqrt(jnp.mean(x * x, axis=-1, keepdims=True) + NORM_EPS)
    return (x * inv) * gain


def _schedule_anchor(values):
    rows, cols = values.shape
    t = values[0:V7X_SUBLANES, :]
    for r0 in range(V7X_SUBLANES, rows, V7X_SUBLANES):
        t = jnp.maximum(t, values[r0:r0 + V7X_SUBLANES, :])
    acc = t[:, 0:V7X_LANES]
    for c0 in range(V7X_LANES, cols, V7X_LANES):
        acc = jnp.maximum(acc, t[:, c0:c0 + V7X_LANES])
    bits = pltpu.bitcast(acc, jnp.uint32)
    return pltpu.bitcast((bits >> 16) >> 16, _F32)


def _add_to_first_tile(array, tile):
    top = jnp.concatenate([array[0:V7X_SUBLANES, 0:V7X_LANES] + tile,
                           array[0:V7X_SUBLANES, V7X_LANES:]], axis=1)
    return jnp.concatenate([top, array[V7X_SUBLANES:, :]], axis=0)


def _resident(shape):
    zeros = (0,) * len(shape)
    return pl.BlockSpec(shape, lambda i: zeros, pipeline_mode=pl.Buffered(1))


_HBM = pl.BlockSpec(memory_space=pl.ANY)


def _next_first_sub_tile(tokens, tile, sub):
    last = tokens // sub - 1
    return pl.BlockSpec((sub, D_MODEL), lambda i: (jnp.minimum((i + 1) * (tile // sub), last), 0))


def _stage_shape(weight_shape):
    rows, cols = weight_shape
    bf16_rows = 2 * V7X_SUBLANES
    fits = [r for r in range(bf16_rows, WEIGHT_STAGE_BYTES // (cols * 4) + 1, bf16_rows)
            if rows % r == 0]
    return (WEIGHT_STAGE_SLOTS, max(fits), cols)


def _load_weights_bf16(jobs, stages, sem_ref):
    chunks, used = [], [0] * len(stages)
    for src, dst, si in jobs:
        slots, rows, _ = stages[si].shape
        for r0 in range(0, dst.shape[0], rows):
            chunks.append((src, dst, si, used[si] % slots, r0, rows))
            used[si] += 1

    def copy(chunk):
        src, _, si, slot, r0, rows = chunk
        return pltpu.make_async_copy(src.at[pl.ds(r0, rows), :], stages[si].at[slot],
                                     sem_ref.at[si, slot])

    ahead = min(stage.shape[0] for stage in stages) - 1
    for k, chunk in enumerate(chunks[:ahead]):
        copy(chunk).start(priority=k % 2)
    for k, chunk in enumerate(chunks):
        if k + ahead < len(chunks):
            copy(chunks[k + ahead]).start(priority=(k + ahead) % 2)
        copy(chunk).wait()
        _, dst, si, slot, r0, rows = chunk
        dst[r0:r0 + rows, :] = stages[si][slot].astype(_BF16)


def _ffn_body(x_ref, xnext_ref, gain_ref, wg_hbm, wu_hbm, wd_hbm, fgain_ref, o_ref,
              h_ref, a_ref, wg_ref, wu_ref, wd_ref, stage_in_ref, stage_out_ref, sem_ref,
              *, layer, final_norm):
    subs = [slice(r0, r0 + FFN_SUB_TILE) for r0 in range(0, x_ref.shape[0], FFN_SUB_TILE)]

    def norm_into(rows, src):
        normed = _rms_norm(src, gain_ref[...])
        h_ref[rows, :] = normed.astype(_BF16)
        return normed

    @pl.when(pl.program_id(0) == 0)
    def _():
        _load_weights_bf16([(wg_hbm.at[layer], wg_ref, 0), (wu_hbm.at[layer], wu_ref, 0),
                            (wd_hbm.at[layer], wd_ref, 1)], [stage_in_ref, stage_out_ref], sem_ref)

    chunks = [(sum(FF_CHUNKS[:i]), sum(FF_CHUNKS[:i + 1])) for i in range(len(FF_CHUNKS))]
    out_blocks = list(range(0, D_MODEL, FFN_OUT_BLOCK))

    def up_item(si, ci):
        rows, (c0, c1) = subs[si], chunks[ci]
        if ci == 0 and si > 0:
            norm_into(rows, x_ref[rows, :])
        h = h_ref[rows, :]
        g = jnp.dot(h, wg_ref[:, c0:c1], preferred_element_type=_F32)
        u = jnp.dot(h, wu_ref[:, c0:c1], preferred_element_type=_F32)
        a_ref[rows, c0:c1] = (g / (1.0 + jnp.exp(-g)) * u).astype(_BF16)

    def first_item_of_next_step():
        norm_into(subs[0], xnext_ref[...])
        up_item(0, 0)

    done = {}

    def down_item(si, n0):
        rows, cols = subs[si], slice(n0, n0 + FFN_OUT_BLOCK)
        y = x_ref[rows, cols] + 0.5 * jnp.dot(a_ref[rows, :], wd_ref[:, cols],
                                              preferred_element_type=_F32)
        if not final_norm:
            o_ref[rows, cols] = y
            return
        done.setdefault(si, []).append(y)
        if n0 == out_blocks[-1]:
            o_ref[rows, :] = _rms_norm(jnp.concatenate(done.pop(si), axis=1), fgain_ref[...])

    @pl.when(pl.program_id(0) == 0)
    def _():
        norm_into(subs[0], x_ref[subs[0], :])
        up_item(0, 0)

    ups = [[functools.partial(up_item, si, ci) for ci in range(1 if si == 0 else 0, len(chunks))]
           for si in range(len(subs))]
    downs = [[functools.partial(down_item, si, n0) for n0 in out_blocks] for si in range(len(subs))]
    every = -(-len(chunks) // len(out_blocks))
    for si in range(len(subs)):
        for k, up in enumerate(ups[si]):
            if si > 0 and k % every == 0 and downs[si - 1]:
                downs[si - 1].pop(0)()
            up()
        if si > 0:
            for down in downs[si - 1]:
                down()
    first_item_of_next_step()
    for down in downs[-1]:
        down()


def _ffn(x, gain, wg, wu, wd, fgain, *, layer, final_norm):
    tokens = x.shape[0]
    tm = FFN_TOKEN_TILE
    assert tokens % tm == 0 and tm % FFN_SUB_TILE == 0 and sum(FF_CHUNKS) == D_FF
    stage_in, stage_out = _stage_shape((D_MODEL, D_FF)), _stage_shape((D_FF, D_MODEL))
    tile_bytes = (2 * tm + FFN_SUB_TILE) * D_MODEL * 4
    resident = (3 * D_MODEL * D_FF * 2 + 2 * D_MODEL * 4
                + 4 * (math.prod(stage_in) + math.prod(stage_out)))
    temps = (tm * (D_MODEL + D_FF) * 2 + 2 * 3 * FFN_SUB_TILE * max(FF_CHUNKS) * 4
             + 2 * FFN_SUB_TILE * D_MODEL * 4)
    row = pl.BlockSpec((tm, D_MODEL), lambda i: (i, 0))
    return pl.pallas_call(
        functools.partial(_ffn_body, layer=layer, final_norm=final_norm),
        grid=(tokens // tm,),
        in_specs=[row, _next_first_sub_tile(tokens, tm, FFN_SUB_TILE), _resident((1, D_MODEL)),
                  _HBM, _HBM, _HBM, _resident((1, D_MODEL))],
        out_specs=row,
        out_shape=jax.ShapeDtypeStruct(x.shape, _F32),
        scratch_shapes=[pltpu.VMEM((tm, D_MODEL), _BF16), pltpu.VMEM((tm, D_FF), _BF16),
                        pltpu.VMEM((D_MODEL, D_FF), _BF16), pltpu.VMEM((D_MODEL, D_FF), _BF16),
                        pltpu.VMEM((D_FF, D_MODEL), _BF16),
                        pltpu.VMEM(stage_in, _F32), pltpu.VMEM(stage_out, _F32),
                        pltpu.SemaphoreType.DMA((2, WEIGHT_STAGE_SLOTS))],
        compiler_params=pltpu.CompilerParams(
            dimension_semantics=("arbitrary",),
            vmem_limit_bytes=_vmem_limit(resident, tile_bytes, temps)),
        name="ffn_final" if final_norm else "ffn",
    )(x, x, gain, wg, wu, wd, fgain)


def _attn_bias_table():
    assert WINDOW == BLOCK
    r = np.arange(BLOCK)[:, None]
    qi = np.arange(BLOCK)[None, :]
    dist = ((qi - r) % BLOCK).astype(np.float32)
    slopes = np.exp2(-8.0 * np.arange(1, N_Q_HEADS + 1, dtype=np.float32) / N_Q_HEADS)
    per_head = -slopes[:, None, None] * dist[None] * np.float32(LOG2E)
    table = [np.concatenate([per_head[GQA_GROUP * g + half], per_head[GQA_GROUP * g + 2 + half]], axis=1)
             for g in range(N_KV_HEADS) for half in range(2)]
    return np.stack(table).astype(np.float32)


def _place_heads(t):
    lo = lax.broadcasted_iota(jnp.int32, t.shape, 1) < HEAD_DIM
    r = pltpu.roll(t, HEAD_DIM, axis=1)
    zero = jnp.zeros_like(t)
    placed = (jnp.where(lo, t, zero), jnp.where(lo, r, zero),
              jnp.where(lo, zero, r), jnp.where(lo, zero, t))
    return [p.astype(_BF16) for p in placed]


def _mixer_body(sink_ref, x_ref, xnext_ref, gain_ref, win_hbm, convw_ref, wout_hbm, bias_ref,
                o_ref, kp_ref, vt_ref, u_ref, mix_ref, s_ref, h_ref, cc_ref,
                win_ref, wout_ref, stage_in_ref, stage_out_ref, sem_ref, *, layer, tiles_per_seq):
    tm = x_ref.shape[0]
    first = pl.program_id(0) % tiles_per_seq == 0

    @pl.when(pl.program_id(0) == 0)
    def _():
        _load_weights_bf16([(win_hbm.at[layer], win_ref, 0), (wout_hbm.at[layer], wout_ref, 1)],
                           [stage_in_ref, stage_out_ref], sem_ref)
        pad_rows = lax.broadcasted_iota(jnp.int32, (VT_ROWS - KV_WIDTH, vt_ref.shape[1]), 0)
        vt_ref[KV_WIDTH:, :] = jnp.where(pad_rows == 0, 1.0, 0.0).astype(_BF16)

    @pl.when(first)
    def _():
        kp_ref[:, 0:BLOCK, :] = jnp.zeros((4, BLOCK, V7X_LANES), _BF16)
        vt_ref[0:KV_WIDTH, 0:BLOCK] = jnp.zeros((KV_WIDTH, BLOCK), _BF16)
        u_ref[0:V7X_SUBLANES, :] = jnp.zeros((V7X_SUBLANES, CONV_WIDTH), _F32)

    sub = MIXER_SUB_TILE
    n_sub = tm // sub
    val = {}

    def rows(s, lo=0, hi=sub):
        return slice(s * sub + lo, s * sub + hi)

    def proj(s, c0, c1):
        return jnp.dot(h_ref[rows(s), :], win_ref[:, c0:c1], preferred_element_type=_F32)

    def conv_in(s, src=None):
        src = x_ref[rows(s), :] if src is None else src
        h_ref[rows(s), :] = _rms_norm(src, gain_ref[...]).astype(_BF16)
        if s == 0:
            cc_ref[...] = proj(s, CB_END, CC_END)
        else:
            val["cc", s] = proj(s, CB_END, CC_END)

    def conv_taps(s):
        u = (cc_ref[...] if s == 0 else val.pop(("cc", s))) * proj(s, CC_END, CH_END)
        base = V7X_SUBLANES + s * sub
        u_ref[base:base + sub, :] = u
        prev = u_ref[base - V7X_SUBLANES:base, :]
        row = lax.broadcasted_iota(jnp.int32, prev.shape, 0)

        def shifted(k):
            rolled = pltpu.roll(u, k, axis=0)
            head = jnp.where(row < k, pltpu.roll(prev, k, axis=0), rolled[0:V7X_SUBLANES, :])
            return jnp.concatenate([head, rolled[V7X_SUBLANES:, :]], axis=0)

        cw = convw_ref[...]
        val["y", s] = cw[0:1, :] * shifted(2) + cw[1:2, :] * shifted(1) + cw[2:3, :] * u

    def step_q(s):
        val["q", s] = (proj(s, 0, Q_END) * (LOG2E / math.sqrt(HEAD_DIM))).astype(_BF16)

    def step_kv(s):
        kv = proj(s, Q_END, KV_END)
        band = slice(BLOCK + s * sub, BLOCK + (s + 1) * sub)
        for idx, p in enumerate(_place_heads(kv[:, :KV_WIDTH])):
            kp_ref[idx, band, :] = p
        vt_ref[0:KV_WIDTH, band] = kv[:, KV_WIDTH:].T.astype(_BF16)

    def conv_gate(s):
        mix_ref[rows(s), ATTN_WIDTH:] = (proj(s, KV_END, CB_END) * val.pop(("y", s))).astype(_BF16)

    projection_steps = (conv_in, conv_taps, step_q, step_kv, conv_gate)

    def out_proj(s, lo, hi):
        r = rows(s, lo, hi)
        o_ref[r, :] = x_ref[r, :] + jnp.dot(mix_ref[r, :], wout_ref[...],
                                            preferred_element_type=_F32)

    prev_mask = jnp.where(first, MASK_VALUE, 0.0).astype(_F32)
    lane = lax.broadcasted_iota(jnp.int32, (1, 2 * BLOCK), 1)
    from_prev = (lax.broadcasted_iota(jnp.int32, (BLOCK, 2 * BLOCK), 0)
                 > lax.broadcasted_iota(jnp.int32, (BLOCK, 2 * BLOCK), 1) % BLOCK)
    nt = (((1,), (1,)), ((), ()))
    chains = [(j, g, half) for j in range(sub // BLOCK) for g in range(N_KV_HEADS)
              for half in range(2)]

    def scores(s, j, g, half):
        q = val["q", s]
        qrows = slice(j * BLOCK, (j + 1) * BLOCK)
        qg = jnp.concatenate([q[qrows, (2 * g) * V7X_LANES:(2 * g + 1) * V7X_LANES],
                              q[qrows, (2 * g + 1) * V7X_LANES:(2 * g + 2) * V7X_LANES]], axis=0)
        band = slice(s * sub + j * BLOCK, s * sub + (j + 2) * BLOCK)
        return lax.dot_general(kp_ref[2 * half + g, band, :], qg, nt,
                               preferred_element_type=_F32)

    def score_phase(s):
        stats = []
        for c, (j, g, half) in enumerate(chains):
            sc = scores(s, j, g, half)
            prev, cur = sc[:BLOCK], sc[BLOCK:]
            if s == 0 and j == 0:
                prev = prev + prev_mask
            sc = jnp.where(from_prev, prev, cur) + bias_ref[2 * g + half]
            s_ref[c] = sc
            sink = LOG2E * jnp.where(lane < BLOCK, sink_ref[GQA_GROUP * g + half],
                                     sink_ref[GQA_GROUP * g + 2 + half])
            stats.append((jnp.maximum(jnp.max(sc, axis=0, keepdims=True), sink), sink))
        val.pop(("q", s))
        return stats

    def value_phase(s, stats, fillers):
        outs = {}
        for c, (j, g, half) in enumerate(chains):
            if c in fillers:
                fillers[c]()
            m, sink = stats[c]
            p = jnp.exp2(s_ref[c] - m)
            p = jnp.concatenate([jnp.where(from_prev, p, 0.0), jnp.where(from_prev, 0.0, p)],
                                axis=0).astype(_BF16)
            band = slice(s * sub + j * BLOCK, s * sub + (j + 2) * BLOCK)
            o = jnp.dot(vt_ref[:, band], p, preferred_element_type=_F32)
            denom = o[KV_WIDTH:KV_WIDTH + 1, :] + jnp.exp2(sink - m)
            outs[half] = o[HEAD_DIM * g:HEAD_DIM * (g + 1), :] * (1.0 / denom)
            if half == 1:
                for pr in range(2):
                    qcols = slice(pr * BLOCK, (pr + 1) * BLOCK)
                    at = jnp.concatenate([outs[0][:, qcols], outs[1][:, qcols]], axis=0)
                    mix_ref[rows(s, j * BLOCK, (j + 1) * BLOCK),
                            (2 * g + pr) * V7X_LANES:(2 * g + pr + 1) * V7X_LANES] = (
                                at.T.astype(_BF16))

    def spread(thunks, first, last):
        return {first + (i * (last - first)) // len(thunks): t for i, t in enumerate(thunks)}

    @pl.when(pl.program_id(0) == 0)
    def _():
        conv_in(0)

    for step in projection_steps[1:]:
        step(0)
    pending = []
    for s in range(n_sub):
        stats = score_phase(s)
        own_half_done = len(chains) // 2 + 1
        if s + 1 < n_sub:
            fillers = spread(pending + [functools.partial(step, s + 1) for step in projection_steps],
                             0, len(chains))
            pending = [functools.partial(out_proj, s, 0, sub // 2),
                       functools.partial(out_proj, s, sub // 2, sub)]
        else:
            fillers = spread(pending, 0, own_half_done) if pending else {}
            fillers[own_half_done] = functools.partial(out_proj, s, 0, sub // 2)
            pending = [functools.partial(out_proj, s, sub // 2, sub)]
        value_phase(s, stats, fillers)
    conv_in(0, xnext_ref[...])
    for thunk in pending:
        thunk()

    u_ref[0:V7X_SUBLANES, :] = u_ref[tm:tm + V7X_SUBLANES, :]
    kp_ref[:, 0:BLOCK, :] = kp_ref[:, tm:tm + BLOCK, :]
    vt_ref[0:KV_WIDTH, 0:BLOCK] = vt_ref[0:KV_WIDTH, tm:tm + BLOCK]


def _mixer(x, gain, w_in, conv_w, sink, w_out, seq_len, *, layer):
    tokens = x.shape[0]
    tm = TOKEN_TILE
    sub = MIXER_SUB_TILE
    assert seq_len % tm == 0 and tm % sub == 0 and sub % BLOCK == 0
    bias = jnp.asarray(_attn_bias_table())
    n_chains = (sub // BLOCK) * N_KV_HEADS * 2
    stage_in, stage_out = _stage_shape((D_MODEL, CH_END)), _stage_shape((MIX_WIDTH, D_MODEL))
    tile_bytes = (2 * tm + sub) * D_MODEL * 4
    resident = (D_MODEL * CH_END * 2 + MIX_WIDTH * D_MODEL * 2 + bias.size * 4
                + (D_MODEL + CONV_K * CONV_WIDTH) * 4
                + 4 * (math.prod(stage_in) + math.prod(stage_out)))
    scratch = ((4 * V7X_LANES + VT_ROWS) * (BLOCK + tm) * 2 + (V7X_SUBLANES + tm) * CONV_WIDTH * 4
               + tm * MIX_WIDTH * 2 + n_chains * 2 * BLOCK * BLOCK * 4 + tm * D_MODEL * 2
               + sub * CONV_WIDTH * 4)
    temps = sub * CH_END * 4 + 8 * 1024 * 1024
    row = pl.BlockSpec((tm, D_MODEL), lambda i: (i, 0))
    return pl.pallas_call(
        functools.partial(_mixer_body, layer=layer, tiles_per_seq=seq_len // tm),
        grid=(tokens // tm,),
        in_specs=[pl.BlockSpec(memory_space=pltpu.SMEM), row, _next_first_sub_tile(tokens, tm, sub),
                  _resident((1, D_MODEL)), _HBM, _resident((CONV_K, CONV_WIDTH)), _HBM,
                  _resident(bias.shape)],
        out_specs=row,
        out_shape=jax.ShapeDtypeStruct(x.shape, _F32),
        scratch_shapes=[pltpu.VMEM((4, BLOCK + tm, V7X_LANES), _BF16),
                        pltpu.VMEM((VT_ROWS, BLOCK + tm), _BF16),
                        pltpu.VMEM((V7X_SUBLANES + tm, CONV_WIDTH), _F32),
                        pltpu.VMEM((tm, MIX_WIDTH), _BF16),
                        pltpu.VMEM((n_chains, BLOCK, 2 * BLOCK), _F32),
                        pltpu.VMEM((tm, D_MODEL), _BF16),
                        pltpu.VMEM((sub, CONV_WIDTH), _F32),
                        pltpu.VMEM((D_MODEL, CH_END), _BF16), pltpu.VMEM((MIX_WIDTH, D_MODEL), _BF16),
                        pltpu.VMEM(stage_in, _F32), pltpu.VMEM(stage_out, _F32),
                        pltpu.SemaphoreType.DMA((2, WEIGHT_STAGE_SLOTS))],
        compiler_params=pltpu.CompilerParams(
            dimension_semantics=("arbitrary",),
            vmem_limit_bytes=_vmem_limit(resident + scratch, tile_bytes, temps)),
        name="mixer",
    )(sink, x, x, gain, w_in, conv_w, w_out, bias)


def kernel(x, ffn1_norm, ffn1_wg, ffn1_wu, ffn1_wd, mix_norm, w_in, conv_w, attn_sink, w_out,
           ffn2_norm, ffn2_wg, ffn2_wu, ffn2_wd, final_norm):
    bsz, seq, d = x.shape
    depth = w_in.shape[0]
    xt = x.reshape(bsz * seq, d)
    fgain = final_norm.reshape(1, d)
    for l in range(depth):
        xt = _ffn(xt, ffn1_norm[l].reshape(1, d), ffn1_wg, ffn1_wu, ffn1_wd, fgain,
                  layer=l, final_norm=False)
        xt = _mixer(xt, mix_norm[l].reshape(1, d), w_in, conv_w[l], attn_sink[l], w_out, seq,
                    layer=l)
        xt = _ffn(xt, ffn2_norm[l].reshape(1, d), ffn2_wg, ffn2_wu, ffn2_wd, fgain,
                  layer=l, final_norm=(l == depth - 1))
    return xt.reshape(bsz, seq, d)
```

```python
import functools
import math

import numpy as np
import jax
import jax.numpy as jnp
from jax import lax
from jax.experimental import pallas as pl
from jax.experimental.pallas import tpu as pltpu

D_MODEL = 1024
HEAD_DIM = 64
N_Q_HEADS = 8
N_KV_HEADS = 2
GQA_GROUP = N_Q_HEADS // N_KV_HEADS
ATTN_WIDTH = N_Q_HEADS * HEAD_DIM
KV_WIDTH = N_KV_HEADS * HEAD_DIM
CONV_WIDTH = 512
MIX_WIDTH = ATTN_WIDTH + CONV_WIDTH
CONV_K = 3
WINDOW = 128
BLOCK = 128
D_FF = 2816
NORM_EPS = 1e-6

Q_END = ATTN_WIDTH
KV_END = Q_END + 2 * KV_WIDTH
CB_END = KV_END + CONV_WIDTH
CC_END = CB_END + CONV_WIDTH
CH_END = CC_END + CONV_WIDTH

V7X_LANES = 128
V7X_SUBLANES = 8
V7X_MXU_DIM = 256
V7X_VMEM_BYTES = 64 * 1024 * 1024

TOKEN_TILE = 1024
MIXER_SUB_TILE = 512
VT_ROWS = KV_WIDTH + 2 * V7X_SUBLANES
FFN_TOKEN_TILE = 1024
FFN_SUB_TILE = 512
FFN_OUT_BLOCK = 2 * V7X_MXU_DIM
FF_CHUNKS = (V7X_MXU_DIM,) * (D_FF // V7X_MXU_DIM)
WEIGHT_STAGE_BYTES = 1536 * 1024
WEIGHT_STAGE_SLOTS = 3
MASK_VALUE = -1e30
LOG2E = math.log2(math.e)

_F32 = jnp.float32
_BF16 = jnp.bfloat16


def _vmem_limit(resident_bytes, tile_bytes, temp_bytes):
    need = resident_bytes + 2 * tile_bytes + temp_bytes
    assert need <= V7X_VMEM_BYTES, need
    return int(need)


def _rms_norm(x, gain):
    inv = lax.rsqrt(jnp.mean(x * x, axis=-1, keepdims=True) + NORM_EPS)
    return (x * inv) * gain


def _schedule_anchor(values):
    rows, cols = values.shape
    t = values[0:V7X_SUBLANES, :]
    for r0 in range(V7X_SUBLANES, rows, V7X_SUBLANES):
        t = jnp.maximum(t, values[r0:r0 + V7X_SUBLANES, :])
    acc = t[:, 0:V7X_LANES]
    for c0 in range(V7X_LANES, cols, V7X_LANES):
        acc = jnp.maximum(acc, t[:, c0:c0 + V7X_LANES])
    bits = pltpu.bitcast(acc, jnp.uint32)
    return pltpu.bitcast((bits >> 16) >> 16, _F32)


def _add_to_first_tile(array, tile):
    top = jnp.concatenate([array[0:V7X_SUBLANES, 0:V7X_LANES] + tile,
                           array[0:V7X_SUBLANES, V7X_LANES:]], axis=1)
    return jnp.concatenate([top, array[V7X_SUBLANES:, :]], axis=0)


def _resident(shape):
    zeros = (0,) * len(shape)
    return pl.BlockSpec(shape, lambda i: zeros, pipeline_mode=pl.Buffered(1))


_HBM = pl.BlockSpec(memory_space=pl.ANY)


def _next_first_sub_tile(tokens, tile, sub):
    last = tokens // sub - 1
    return pl.BlockSpec((sub, D_MODEL), lambda i: (jnp.minimum((i + 1) * (tile // sub), last), 0))


def _stage_shape(weight_shape):
    rows, cols = weight_shape
    bf16_rows = 2 * V7X_SUBLANES
    fits = [r for r in range(bf16_rows, WEIGHT_STAGE_BYTES // (cols * 4) + 1, bf16_rows)
            if rows % r == 0]
    return (WEIGHT_STAGE_SLOTS, max(fits), cols)


def _load_weights_bf16(jobs, stages, sem_ref):
    chunks, used = [], [0] * len(stages)
    for src, dst, si in jobs:
        slots, rows, _ = stages[si].shape
        for r0 in range(0, dst.shape[0], rows):
            chunks.append((src, dst, si, used[si] % slots, r0, rows))
            used[si] += 1

    def copy(chunk):
        src, _, si, slot, r0, rows = chunk
        return pltpu.make_async_copy(src.at[pl.ds(r0, rows), :], stages[si].at[slot],
                                     sem_ref.at[si, slot])

    ahead = min(stage.shape[0] for stage in stages) - 1
    for k, chunk in enumerate(chunks[:ahead]):
        copy(chunk).start(priority=k % 2)
    for k, chunk in enumerate(chunks):
        if k + ahead < len(chunks):
            copy(chunks[k + ahead]).start(priority=(k + ahead) % 2)
        copy(chunk).wait()
        _, dst, si, slot, r0, rows = chunk
        dst[r0:r0 + rows, :] = stages[si][slot].astype(_BF16)


def _ffn_body(x_ref, xnext_ref, gain_ref, wg_hbm, wu_hbm, wd_hbm, fgain_ref, o_ref,
              h_ref, a_ref, wg_ref, wu_ref, wd_ref, stage_in_ref, stage_out_ref, sem_ref,
              *, layer, final_norm):
    subs = [slice(r0, r0 + FFN_SUB_TILE) for r0 in range(0, x_ref.shape[0], FFN_SUB_TILE)]

    def norm_into(rows, src):
        normed = _rms_norm(src, gain_ref[...])
        h_ref[rows, :] = normed.astype(_BF16)
        return normed

    @pl.when(pl.program_id(0) == 0)
    def _():
        _load_weights_bf16([(wg_hbm.at[layer], wg_ref, 0), (wu_hbm.at[layer], wu_ref, 0),
                            (wd_hbm.at[layer], wd_ref, 1)], [stage_in_ref, stage_out_ref], sem_ref)

    chunks = [(sum(FF_CHUNKS[:i]), sum(FF_CHUNKS[:i + 1])) for i in range(len(FF_CHUNKS))]
    out_blocks = list(range(0, D_MODEL, FFN_OUT_BLOCK))

    def up_item(si, ci):
        rows, (c0, c1) = subs[si], chunks[ci]
        if ci == 0 and si > 0:
            norm_into(rows, x_ref[rows, :])
        h = h_ref[rows, :]
        g = jnp.dot(h, wg_ref[:, c0:c1], preferred_element_type=_F32)
        u = jnp.dot(h, wu_ref[:, c0:c1], preferred_element_type=_F32)
        a_ref[rows, c0:c1] = (g / (1.0 + jnp.exp(-g)) * u).astype(_BF16)

    def first_item_of_next_step():
        norm_into(subs[0], xnext_ref[...])
        up_item(0, 0)

    done = {}

    def down_item(si, n0):
        rows, cols = subs[si], slice(n0, n0 + FFN_OUT_BLOCK)
        y = x_ref[rows, cols] + 0.5 * jnp.dot(a_ref[rows, :], wd_ref[:, cols],
                                              preferred_element_type=_F32)
        if not final_norm:
            o_ref[rows, cols] = y
            return
        done.setdefault(si, []).append(y)
        if n0 == out_blocks[-1]:
            o_ref[rows, :] = _rms_norm(jnp.concatenate(done.pop(si), axis=1), fgain_ref[...])

    @pl.when(pl.program_id(0) == 0)
    def _():
        norm_into(subs[0], x_ref[subs[0], :])
        up_item(0, 0)

    ups = [[functools.partial(up_item, si, ci) for ci in range(1 if si == 0 else 0, len(chunks))]
           for si in range(len(subs))]
    downs = [[functools.partial(down_item, si, n0) for n0 in out_blocks] for si in range(len(subs))]
    every = -(-len(chunks) // len(out_blocks))
    for si in range(len(subs)):
        for k, up in enumerate(ups[si]):
            if si > 0 and k % every == 0 and downs[si - 1]:
                downs[si - 1].pop(0)()
            up()
        if si > 0:
            for down in downs[si - 1]:
                down()
    first_item_of_next_step()
    for down in downs[-1]:
        down()


def _ffn(x, gain, wg, wu, wd, fgain, *, layer, final_norm):
    tokens = x.shape[0]
    tm = FFN_TOKEN_TILE
    assert tokens % tm == 0 and tm % FFN_SUB_TILE == 0 and sum(FF_CHUNKS) == D_FF
    stage_in, stage_out = _stage_shape((D_MODEL, D_FF)), _stage_shape((D_FF, D_MODEL))
    tile_bytes = (2 * tm + FFN_SUB_TILE) * D_MODEL * 4
    resident = (3 * D_MODEL * D_FF * 2 + 2 * D_MODEL * 4
                + 4 * (math.prod(stage_in) + math.prod(stage_out)))
    temps = (tm * (D_MODEL + D_FF) * 2 + 2 * 3 * FFN_SUB_TILE * max(FF_CHUNKS) * 4
             + 2 * FFN_SUB_TILE * D_MODEL * 4)
    row = pl.BlockSpec((tm, D_MODEL), lambda i: (i, 0))
    return pl.pallas_call(
        functools.partial(_ffn_body, layer=layer, final_norm=final_norm),
        grid=(tokens // tm,),
        in_specs=[row, _next_first_sub_tile(tokens, tm, FFN_SUB_TILE), _resident((1, D_MODEL)),
                  _HBM, _HBM, _HBM, _resident((1, D_MODEL))],
        out_specs=row,
        out_shape=jax.ShapeDtypeStruct(x.shape, _F32),
        scratch_shapes=[pltpu.VMEM((tm, D_MODEL), _BF16), pltpu.VMEM((tm, D_FF), _BF16),
                        pltpu.VMEM((D_MODEL, D_FF), _BF16), pltpu.VMEM((D_MODEL, D_FF), _BF16),
                        pltpu.VMEM((D_FF, D_MODEL), _BF16),
                        pltpu.VMEM(stage_in, _F32), pltpu.VMEM(stage_out, _F32),
                        pltpu.SemaphoreType.DMA((2, WEIGHT_STAGE_SLOTS))],
        compiler_params=pltpu.CompilerParams(
            dimension_semantics=("arbitrary",),
            vmem_limit_bytes=_vmem_limit(resident, tile_bytes, temps)),
        name="ffn_final" if final_norm else "ffn",
    )(x, x, gain, wg, wu, wd, fgain)


def _attn_bias_table():
    assert WINDOW == BLOCK
    r = np.arange(BLOCK)[:, None]
    qi = np.arange(BLOCK)[None, :]
    dist = ((qi - r) % BLOCK).astype(np.float32)
    slopes = np.exp2(-8.0 * np.arange(1, N_Q_HEADS + 1, dtype=np.float32) / N_Q_HEADS)
    per_head = -slopes[:, None, None] * dist[None] * np.float32(LOG2E)
    table = [np.concatenate([per_head[GQA_GROUP * g + half], per_head[GQA_GROUP * g + 2 + half]], axis=1)
             for g in range(N_KV_HEADS) for half in range(2)]
    return np.stack(table).astype(np.float32)


def _place_heads(t):
    lo = lax.broadcasted_iota(jnp.int32, t.shape, 1) < HEAD_DIM
    r = pltpu.roll(t, HEAD_DIM, axis=1)
    zero = jnp.zeros_like(t)
    placed = (jnp.where(lo, t, zero), jnp.where(lo, r, zero),
              jnp.where(lo, zero, r), jnp.where(lo, zero, t))
    return [p.astype(_BF16) for p in placed]


def _mixer_body(sink_ref, x_ref, xnext_ref, gain_ref, win_hbm, convw_ref, wout_hbm, bias_ref,
                o_ref, kp_ref, vt_ref, u_ref, mix_ref, s_ref, h_ref, cc_ref,
                win_ref, wout_ref, stage_in_ref, stage_out_ref, sem_ref, *, layer, tiles_per_seq):
    tm = x_ref.shape[0]
    first = pl.program_id(0) % tiles_per_seq == 0

    @pl.when(pl.program_id(0) == 0)
    def _():
        _load_weights_bf16([(win_hbm.at[layer], win_ref, 0), (wout_hbm.at[layer], wout_ref, 1)],
                           [stage_in_ref, stage_out_ref], sem_ref)
        pad_rows = lax.broadcasted_iota(jnp.int32, (VT_ROWS - KV_WIDTH, vt_ref.shape[1]), 0)
        vt_ref[KV_WIDTH:, :] = jnp.where(pad_rows == 0, 1.0, 0.0).astype(_BF16)

    @pl.when(first)
    def _():
        kp_ref[:, 0:BLOCK, :] = jnp.zeros((4, BLOCK, V7X_LANES), _BF16)
        vt_ref[0:KV_WIDTH, 0:BLOCK] = jnp.zeros((KV_WIDTH, BLOCK), _BF16)
        u_ref[0:V7X_SUBLANES, :] = jnp.zeros((V7X_SUBLANES, CONV_WIDTH), _F32)

    sub = MIXER_SUB_TILE
    n_sub = tm // sub
    val = {}

    def rows(s, lo=0, hi=sub):
        return slice(s * sub + lo, s * sub + hi)

    def proj(s, c0, c1):
        return jnp.dot(h_ref[rows(s), :], win_ref[:, c0:c1], preferred_element_type=_F32)

    def conv_in(s, src=None):
        src = x_ref[rows(s), :] if src is None else src
        h_ref[rows(s), :] = _rms_norm(src, gain_ref[...]).astype(_BF16)
        if s == 0:
            cc_ref[...] = proj(s, CB_END, CC_END)
        else:
            val["cc", s] = proj(s, CB_END, CC_END)

    def conv_taps(s):
        u = (cc_ref[...] if s == 0 else val.pop(("cc", s))) * proj(s, CC_END, CH_END)
        base = V7X_SUBLANES + s * sub
        u_ref[base:base + sub, :] = u
        prev = u_ref[base - V7X_SUBLANES:base, :]
        row = lax.broadcasted_iota(jnp.int32, prev.shape, 0)

        def shifted(k):
            rolled = pltpu.roll(u, k, axis=0)
            head = jnp.where(row < k, pltpu.roll(prev, k, axis=0), rolled[0:V7X_SUBLANES, :])
            return jnp.concatenate([head, rolled[V7X_SUBLANES:, :]], axis=0)

        cw = convw_ref[...]
        val["y", s] = cw[0:1, :] * shifted(2) + cw[1:2, :] * shifted(1) + cw[2:3, :] * u

    def step_q(s):
        val["q", s] = (proj(s, 0, Q_END) * (LOG2E / math.sqrt(HEAD_DIM))).astype(_BF16)

    def step_kv(s):
        kv = proj(s, Q_END, KV_END)
        band = slice(BLOCK + s * sub, BLOCK + (s + 1) * sub)
        for idx, p in enumerate(_place_heads(kv[:, :KV_WIDTH])):
            kp_ref[idx, band, :] = p
        vt_ref[0:KV_WIDTH, band] = kv[:, KV_WIDTH:].T.astype(_BF16)

    def conv_gate(s):
        mix_ref[rows(s), ATTN_WIDTH:] = (proj(s, KV_END, CB_END) * val.pop(("y", s))).astype(_BF16)

    projection_steps = (conv_in, conv_taps, step_q, step_kv, conv_gate)

    def out_proj(s, lo, hi):
        r = rows(s, lo, hi)
        o_ref[r, :] = x_ref[r, :] + jnp.dot(mix_ref[r, :], wout_ref[...],
                                            preferred_element_type=_F32)

    prev_mask = jnp.where(first, MASK_VALUE, 0.0).astype(_F32)
    lane = lax.broadcasted_iota(jnp.int32, (1, 2 * BLOCK), 1)
    from_prev = (lax.broadcasted_iota(jnp.int32, (BLOCK, 2 * BLOCK), 0)
                 > lax.broadcasted_iota(jnp.int32, (BLOCK, 2 * BLOCK), 1) % BLOCK)
    nt = (((1,), (1,)), ((), ()))
    chains = [(j, g, half) for j in range(sub // BLOCK) for g in range(N_KV_HEADS)
              for half in range(2)]

    def scores(s, j, g, half):
        q = val["q", s]
        qrows = slice(j * BLOCK, (j + 1) * BLOCK)
        qg = jnp.concatenate([q[qrows, (2 * g) * V7X_LANES:(2 * g + 1) * V7X_LANES],
                              q[qrows, (2 * g + 1) * V7X_LANES:(2 * g + 2) * V7X_LANES]], axis=0)
        band = slice(s * sub + j * BLOCK, s * sub + (j + 2) * BLOCK)
        return lax.dot_general(kp_ref[2 * half + g, band, :], qg, nt,
                               preferred_element_type=_F32)

    def score_phase(s):
        stats = []
        for c, (j, g, half) in enumerate(chains):
            sc = scores(s, j, g, half)
            prev, cur = sc[:BLOCK], sc[BLOCK:]
            if s == 0 and j == 0:
                prev = prev + prev_mask
            sc = jnp.where(from_prev, prev, cur) + bias_ref[2 * g + half]
            s_ref[c] = sc
            sink = LOG2E * jnp.where(lane < BLOCK, sink_ref[GQA_GROUP * g + half],
                                     sink_ref[GQA_GROUP * g + 2 + half])
            stats.append((jnp.maximum(jnp.max(sc, axis=0, keepdims=True), sink), sink))
        val.pop(("q", s))
        return stats

    def value_phase(s, stats, fillers):
        outs = {}
        for c, (j, g, half) in enumerate(chains):
            if c in fillers:
                fillers[c]()
            m, sink = stats[c]
            p = jnp.exp2(s_ref[c] - m)
            p = jnp.concatenate([jnp.where(from_prev, p, 0.0), jnp.where(from_prev, 0.0, p)],
                                axis=0).astype(_BF16)
            band = slice(s * sub + j * BLOCK, s * sub + (j + 2) * BLOCK)
            o = jnp.dot(vt_ref[:, band], p, preferred_element_type=_F32)
            denom = o[KV_WIDTH:KV_WIDTH + 1, :] + jnp.exp2(sink - m)
            outs[half] = o[HEAD_DIM * g:HEAD_DIM * (g + 1), :] * (1.0 / denom)
            if half == 1:
                for pr in range(2):
                    qcols = slice(pr * BLOCK, (pr + 1) * BLOCK)
                    at = jnp.concatenate([outs[0][:, qcols], outs[1][:, qcols]], axis=0)
                    mix_ref[rows(s, j * BLOCK, (j + 1) * BLOCK),
                            (2 * g + pr) * V7X_LANES:(2 * g + pr + 1) * V7X_LANES] = (
                                at.T.astype(_BF16))

    def spread(thunks, first, last):
        return {first + (i * (last - first)) // len(thunks): t for i, t in enumerate(thunks)}

    @pl.when(pl.program_id(0) == 0)
    def _():
        conv_in(0)

    for step in projection_steps[1:]:
        step(0)
    pending = []
    for s in range(n_sub):
        stats = score_phase(s)
        own_half_done = len(chains) // 2 + 1
        if s + 1 < n_sub:
            fillers = spread(pending + [functools.partial(step, s + 1) for step in projection_steps],
                             0, len(chains))
            pending = [functools.partial(out_proj, s, 0, sub // 2),
                       functools.partial(out_proj, s, sub // 2, sub)]
        else:
            fillers = spread(pending, 0, own_half_done) if pending else {}
            fillers[own_half_done] = functools.partial(out_proj, s, 0, sub // 2)
            pending = [functools.partial(out_proj, s, sub // 2, sub)]
        value_phase(s, stats, fillers)
    conv_in(0, xnext_ref[...])
    for thunk in pending:
        thunk()

    u_ref[0:V7X_SUBLANES, :] = u_ref[tm:tm + V7X_SUBLANES, :]
    kp_ref[:, 0:BLOCK, :] = kp_ref[:, tm:tm + BLOCK, :]
    vt_ref[0:KV_WIDTH, 0:BLOCK] = vt_ref[0:KV_WIDTH, tm:tm + BLOCK]


def _mixer(x, gain, w_in, conv_w, sink, w_out, seq_len, *, layer):
    tokens = x.shape[0]
    tm = TOKEN_TILE
    sub = MIXER_SUB_TILE
    assert seq_len % tm == 0 and tm % sub == 0 and sub % BLOCK == 0
    bias = jnp.asarray(_attn_bias_table())
    n_chains = (sub // BLOCK) * N_KV_HEADS * 2
    stage_in, stage_out = _stage_shape((D_MODEL, CH_END)), _stage_shape((MIX_WIDTH, D_MODEL))
    tile_bytes = (2 * tm + sub) * D_MODEL * 4
    resident = (D_MODEL * CH_END * 2 + MIX_WIDTH * D_MODEL * 2 + bias.size * 4
                + (D_MODEL + CONV_K * CONV_WIDTH) * 4
                + 4 * (math.prod(stage_in) + math.prod(stage_out)))
    scratch = ((4 * V7X_LANES + VT_ROWS) * (BLOCK + tm) * 2 + (V7X_SUBLANES + tm) * CONV_WIDTH * 4
               + tm * MIX_WIDTH * 2 + n_chains * 2 * BLOCK * BLOCK * 4 + tm * D_MODEL * 2
               + sub * CONV_WIDTH * 4)
    temps = sub * CH_END * 4 + 8 * 1024 * 1024
    row = pl.BlockSpec((tm, D_MODEL), lambda i: (i, 0))
    return pl.pallas_call(
        functools.partial(_mixer_body, layer=layer, tiles_per_seq=seq_len // tm),
        grid=(tokens // tm,),
        in_specs=[pl.BlockSpec(memory_space=pltpu.SMEM), row, _next_first_sub_tile(tokens, tm, sub),
                  _resident((1, D_MODEL)), _HBM, _resident((CONV_K, CONV_WIDTH)), _HBM,
                  _resident(bias.shape)],
        out_specs=row,
        out_shape=jax.ShapeDtypeStruct(x.shape, _F32),
        scratch_shapes=[pltpu.VMEM((4, BLOCK + tm, V7X_LANES), _BF16),
                        pltpu.VMEM((VT_ROWS, BLOCK + tm), _BF16),
                        pltpu.VMEM((V7X_SUBLANES + tm, CONV_WIDTH), _F32),
                        pltpu.VMEM((tm, MIX_WIDTH), _BF16),
                        pltpu.VMEM((n_chains, BLOCK, 2 * BLOCK), _F32),
                        pltpu.VMEM((tm, D_MODEL), _BF16),
                        pltpu.VMEM((sub, CONV_WIDTH), _F32),
                        pltpu.VMEM((D_MODEL, CH_END), _BF16), pltpu.VMEM((MIX_WIDTH, D_MODEL), _BF16),
                        pltpu.VMEM(stage_in, _F32), pltpu.VMEM(stage_out, _F32),
                        pltpu.SemaphoreType.DMA((2, WEIGHT_STAGE_SLOTS))],
        compiler_params=pltpu.CompilerParams(
            dimension_semantics=("arbitrary",),
            vmem_limit_bytes=_vmem_limit(resident + scratch, tile_bytes, temps)),
        name="mixer",
    )(sink, x, x, gain, w_in, conv_w, w_out, bias)


def kernel(x, ffn1_norm, ffn1_wg, ffn1_wu, ffn1_wd, mix_norm, w_in, conv_w, attn_sink, w_out,
           ffn2_norm, ffn2_wg, ffn2_wu, ffn2_wd, final_norm):
    bsz, seq, d = x.shape
    depth = w_in.shape[0]
    xt = x.reshape(bsz * seq, d)
    fgain = final_norm.reshape(1, d)
    for l in range(depth):
        xt = _ffn(xt, ffn1_norm[l].reshape(1, d), ffn1_wg, ffn1_wu, ffn1_wd, fgain,
                  layer=l, final_norm=False)
        xt = _mixer(xt, mix_norm[l].reshape(1, d), w_in, conv_w[l], attn_sink[l], w_out, seq,
                    layer=l)
        xt = _ffn(xt, ffn2_norm[l].reshape(1, d), ffn2_wg, ffn2_wu, ffn2_wd, fgain,
                  layer=l, final_norm=(l == depth - 1))
    return xt.reshape(bsz, seq, d)
```

```python
import functools
import math

import numpy as np
import jax
import jax.numpy as jnp
from jax import lax
from jax.experimental import pallas as pl
from jax.experimental.pallas import tpu as pltpu

D_MODEL = 1024
HEAD_DIM = 64
N_Q_HEADS = 8
N_KV_HEADS = 2
GQA_GROUP = N_Q_HEADS // N_KV_HEADS
ATTN_WIDTH = N_Q_HEADS * HEAD_DIM
KV_WIDTH = N_KV_HEADS * HEAD_DIM
CONV_WIDTH = 512
MIX_WIDTH = ATTN_WIDTH + CONV_WIDTH
CONV_K = 3
WINDOW = 128
BLOCK = 128
D_FF = 2816
NORM_EPS = 1e-6

Q_END = ATTN_WIDTH
KV_END = Q_END + 2 * KV_WIDTH
CB_END = KV_END + CONV_WIDTH
CC_END = CB_END + CONV_WIDTH
CH_END = CC_END + CONV_WIDTH

V7X_LANES = 128
V7X_SUBLANES = 8
V7X_MXU_DIM = 256
V7X_VMEM_BYTES = 64 * 1024 * 1024

TOKEN_TILE = 1024
MIXER_SUB_TILE = 512
VT_ROWS = KV_WIDTH + 2 * V7X_SUBLANES
FFN_TOKEN_TILE = 1024
FFN_SUB_TILE = 512
FFN_OUT_BLOCK = 2 * V7X_MXU_DIM
FF_CHUNKS = (V7X_MXU_DIM,) * (D_FF // V7X_MXU_DIM)
WEIGHT_STAGE_BYTES = 1536 * 1024
WEIGHT_STAGE_SLOTS = 3
MASK_VALUE = -1e30
LOG2E = math.log2(math.e)

_F32 = jnp.float32
_BF16 = jnp.bfloat16


def _vmem_limit(resident_bytes, tile_bytes, temp_bytes):
    need = resident_bytes + 2 * tile_bytes + temp_bytes
    assert need <= V7X_VMEM_BYTES, need
    return int(need)


def _rms_norm(x, gain):
    inv = lax.rsqrt(jnp.mean(x * x, axis=-1, keepdims=True) + NORM_EPS)
    return (x * inv) * gain


def _resident(shape):
    zeros = (0,) * len(shape)
    return pl.BlockSpec(shape, lambda i: zeros, pipeline_mode=pl.Buffered(1))


_HBM = pl.BlockSpec(memory_space=pl.ANY)


def _next_first_sub_tile(tokens, tile, sub):
    last = tokens // sub - 1
    return pl.BlockSpec((sub, D_MODEL), lambda i: (jnp.minimum((i + 1) * (tile // sub), last), 0))


def _stage_shape(weight_shape):
    rows, cols = weight_shape
    bf16_rows = 2 * V7X_SUBLANES
    fits = [r for r in range(bf16_rows, WEIGHT_STAGE_BYTES // (cols * 4) + 1, bf16_rows)
            if rows % r == 0]
    return (WEIGHT_STAGE_SLOTS, max(fits), cols)


def _load_weights_bf16(jobs, stages, sem_ref):
    chunks, used = [], [0] * len(stages)
    for src, dst, si in jobs:
        slots, rows, _ = stages[si].shape
        for r0 in range(0, dst.shape[0], rows):
            chunks.append((src, dst, si, used[si] % slots, r0, rows))
            used[si] += 1

    def copy(chunk):
        src, _, si, slot, r0, rows = chunk
        return pltpu.make_async_copy(src.at[pl.ds(r0, rows), :], stages[si].at[slot],
                                     sem_ref.at[si, slot])

    ahead = min(stage.shape[0] for stage in stages) - 1
    for chunk in chunks[:ahead]:
        copy(chunk).start()
    for k, chunk in enumerate(chunks):
        if k + ahead < len(chunks):
            copy(chunks[k + ahead]).start()
        copy(chunk).wait()
        _, dst, si, slot, r0, rows = chunk
        dst[r0:r0 + rows, :] = stages[si][slot].astype(_BF16)


def _ffn_body(x_ref, xnext_ref, gain_ref, wg_hbm, wu_hbm, wd_hbm, fgain_ref, o_ref,
              h_ref, a_ref, wg_ref, wu_ref, wd_ref, stage_in_ref, stage_out_ref, sem_ref,
              *, layer, final_norm):
    subs = [slice(r0, r0 + FFN_SUB_TILE) for r0 in range(0, x_ref.shape[0], FFN_SUB_TILE)]

    def norm_into(rows, src):
        h_ref[rows, :] = _rms_norm(src, gain_ref[...]).astype(_BF16)

    @pl.when(pl.program_id(0) == 0)
    def _():
        _load_weights_bf16([(wg_hbm.at[layer], wg_ref, 0), (wu_hbm.at[layer], wu_ref, 0),
                            (wd_hbm.at[layer], wd_ref, 1)], [stage_in_ref, stage_out_ref], sem_ref)

    chunks = [(sum(FF_CHUNKS[:i]), sum(FF_CHUNKS[:i + 1])) for i in range(len(FF_CHUNKS))]
    out_blocks = list(range(0, D_MODEL, FFN_OUT_BLOCK))

    def up_item(si, ci):
        rows, (c0, c1) = subs[si], chunks[ci]
        if ci == 0 and si > 0:
            norm_into(rows, x_ref[rows, :])
        h = h_ref[rows, :]
        g = jnp.dot(h, wg_ref[:, c0:c1], preferred_element_type=_F32)
        u = jnp.dot(h, wu_ref[:, c0:c1], preferred_element_type=_F32)
        a_ref[rows, c0:c1] = (g / (1.0 + jnp.exp(-g)) * u).astype(_BF16)

    def first_item_of_next_step():
        norm_into(subs[0], xnext_ref[...])
        up_item(0, 0)

    done = {}

    def down_item(si, n0):
        rows, cols = subs[si], slice(n0, n0 + FFN_OUT_BLOCK)
        y = x_ref[rows, cols] + 0.5 * jnp.dot(a_ref[rows, :], wd_ref[:, cols],
                                              preferred_element_type=_F32)
        if not final_norm:
            o_ref[rows, cols] = y
            return
        done.setdefault(si, []).append(y)
        if n0 == out_blocks[-1]:
            o_ref[rows, :] = _rms_norm(jnp.concatenate(done.pop(si), axis=1), fgain_ref[...])

    @pl.when(pl.program_id(0) == 0)
    def _():
        norm_into(subs[0], x_ref[subs[0], :])
        up_item(0, 0)

    ups = [[functools.partial(up_item, si, ci) for ci in range(1 if si == 0 else 0, len(chunks))]
           for si in range(len(subs))]
    downs = [[functools.partial(down_item, si, n0) for n0 in out_blocks] for si in range(len(subs))]
    every = -(-len(chunks) // len(out_blocks))
    for si in range(len(subs)):
        for k, up in enumerate(ups[si]):
            if si > 0 and k % every == 0 and downs[si - 1]:
                downs[si - 1].pop(0)()
            up()
        if si > 0:
            for down in downs[si - 1]:
                down()
    first_item_of_next_step()
    for down in downs[-1]:
        down()


def _ffn(x, gain, wg, wu, wd, fgain, *, layer, final_norm):
    tokens = x.shape[0]
    tm = FFN_TOKEN_TILE
    assert tokens % tm == 0 and tm % FFN_SUB_TILE == 0 and sum(FF_CHUNKS) == D_FF
    stage_in, stage_out = _stage_shape((D_MODEL, D_FF)), _stage_shape((D_FF, D_MODEL))
    tile_bytes = (2 * tm + FFN_SUB_TILE) * D_MODEL * 4
    resident = (3 * D_MODEL * D_FF * 2 + 2 * D_MODEL * 4
                + 4 * (math.prod(stage_in) + math.prod(stage_out)))
    temps = (tm * (D_MODEL + D_FF) * 2 + 2 * 3 * FFN_SUB_TILE * max(FF_CHUNKS) * 4
             + 2 * FFN_SUB_TILE * D_MODEL * 4)
    row = pl.BlockSpec((tm, D_MODEL), lambda i: (i, 0))
    return pl.pallas_call(
        functools.partial(_ffn_body, layer=layer, final_norm=final_norm),
        grid=(tokens // tm,),
        in_specs=[row, _next_first_sub_tile(tokens, tm, FFN_SUB_TILE), _resident((1, D_MODEL)),
                  _HBM, _HBM, _HBM, _resident((1, D_MODEL))],
        out_specs=row,
        out_shape=jax.ShapeDtypeStruct(x.shape, _F32),
        scratch_shapes=[pltpu.VMEM((tm, D_MODEL), _BF16), pltpu.VMEM((tm, D_FF), _BF16),
                        pltpu.VMEM((D_MODEL, D_FF), _BF16), pltpu.VMEM((D_MODEL, D_FF), _BF16),
                        pltpu.VMEM((D_FF, D_MODEL), _BF16),
                        pltpu.VMEM(stage_in, _F32), pltpu.VMEM(stage_out, _F32),
                        pltpu.SemaphoreType.DMA((2, WEIGHT_STAGE_SLOTS))],
        compiler_params=pltpu.CompilerParams(
            dimension_semantics=("arbitrary",),
            vmem_limit_bytes=_vmem_limit(resident, tile_bytes, temps)),
        name="ffn_final" if final_norm else "ffn",
    )(x, x, gain, wg, wu, wd, fgain)


def _attn_bias_table():
    assert WINDOW == BLOCK
    r = np.arange(BLOCK)[:, None]
    qi = np.arange(BLOCK)[None, :]
    dist = ((qi - r) % BLOCK).astype(np.float32)
    slopes = np.exp2(-8.0 * np.arange(1, N_Q_HEADS + 1, dtype=np.float32) / N_Q_HEADS)
    per_head = -slopes[:, None, None] * dist[None] * np.float32(LOG2E)
    table = [np.concatenate([per_head[GQA_GROUP * g + half], per_head[GQA_GROUP * g + 2 + half]], axis=1)
             for g in range(N_KV_HEADS) for half in range(2)]
    return np.stack(table).astype(np.float32)


def _place_heads(t):
    lo = lax.broadcasted_iota(jnp.int32, t.shape, 1) < HEAD_DIM
    r = pltpu.roll(t, HEAD_DIM, axis=1)
    zero = jnp.zeros_like(t)
    placed = (jnp.where(lo, t, zero), jnp.where(lo, r, zero),
              jnp.where(lo, zero, r), jnp.where(lo, zero, t))
    return [p.astype(_BF16) for p in placed]


def _mixer_body(sink_ref, x_ref, xnext_ref, gain_ref, win_hbm, convw_ref, wout_hbm, bias_ref,
                o_ref, kp_ref, vt_ref, u_ref, mix_ref, s_ref, h_ref, cc_ref,
                win_ref, wout_ref, stage_in_ref, stage_out_ref, sem_ref, *, layer, tiles_per_seq):
    tm = x_ref.shape[0]
    first = pl.program_id(0) % tiles_per_seq == 0

    @pl.when(pl.program_id(0) == 0)
    def _():
        _load_weights_bf16([(win_hbm.at[layer], win_ref, 0), (wout_hbm.at[layer], wout_ref, 1)],
                           [stage_in_ref, stage_out_ref], sem_ref)
        pad_rows = lax.broadcasted_iota(jnp.int32, (VT_ROWS - KV_WIDTH, vt_ref.shape[1]), 0)
        vt_ref[KV_WIDTH:, :] = jnp.where(pad_rows == 0, 1.0, 0.0).astype(_BF16)

    @pl.when(first)
    def _():
        kp_ref[:, 0:BLOCK, :] = jnp.zeros((4, BLOCK, V7X_LANES), _BF16)
        vt_ref[0:KV_WIDTH, 0:BLOCK] = jnp.zeros((KV_WIDTH, BLOCK), _BF16)
        u_ref[0:V7X_SUBLANES, :] = jnp.zeros((V7X_SUBLANES, CONV_WIDTH), _F32)

    sub = MIXER_SUB_TILE
    n_sub = tm // sub
    val = {}

    def rows(s, lo=0, hi=sub):
        return slice(s * sub + lo, s * sub + hi)

    def proj(s, c0, c1):
        return jnp.dot(h_ref[rows(s), :], win_ref[:, c0:c1], preferred_element_type=_F32)

    def conv_in(s, src=None):
        src = x_ref[rows(s), :] if src is None else src
        h_ref[rows(s), :] = _rms_norm(src, gain_ref[...]).astype(_BF16)
        if s == 0:
            cc_ref[...] = proj(s, CB_END, CC_END)
        else:
            val["cc", s] = proj(s, CB_END, CC_END)

    def conv_taps(s):
        u = (cc_ref[...] if s == 0 else val.pop(("cc", s))) * proj(s, CC_END, CH_END)
        base = V7X_SUBLANES + s * sub
        u_ref[base:base + sub, :] = u
        prev = u_ref[base - V7X_SUBLANES:base, :]
        row = lax.broadcasted_iota(jnp.int32, prev.shape, 0)

        def shifted(k):
            rolled = pltpu.roll(u, k, axis=0)
            head = jnp.where(row < k, pltpu.roll(prev, k, axis=0), rolled[0:V7X_SUBLANES, :])
            return jnp.concatenate([head, rolled[V7X_SUBLANES:, :]], axis=0)

        cw = convw_ref[...]
        val["y", s] = cw[0:1, :] * shifted(2) + cw[1:2, :] * shifted(1) + cw[2:3, :] * u

    def step_q(s):
        val["q", s] = (proj(s, 0, Q_END) * (LOG2E / math.sqrt(HEAD_DIM))).astype(_BF16)

    def step_kv(s):
        kv = proj(s, Q_END, KV_END)
        band = slice(BLOCK + s * sub, BLOCK + (s + 1) * sub)
        for idx, p in enumerate(_place_heads(kv[:, :KV_WIDTH])):
            kp_ref[idx, band, :] = p
        vt_ref[0:KV_WIDTH, band] = kv[:, KV_WIDTH:].T.astype(_BF16)

    def conv_gate(s):
        mix_ref[rows(s), ATTN_WIDTH:] = (proj(s, KV_END, CB_END) * val.pop(("y", s))).astype(_BF16)

    projection_steps = (conv_in, conv_taps, step_q, step_kv, conv_gate)

    def out_proj(s, lo, hi):
        r = rows(s, lo, hi)
        o_ref[r, :] = x_ref[r, :] + jnp.dot(mix_ref[r, :], wout_ref[...],
                                            preferred_element_type=_F32)

    prev_mask = jnp.where(first, MASK_VALUE, 0.0).astype(_F32)
    lane = lax.broadcasted_iota(jnp.int32, (1, 2 * BLOCK), 1)
    from_prev = (lax.broadcasted_iota(jnp.int32, (BLOCK, 2 * BLOCK), 0)
                 > lax.broadcasted_iota(jnp.int32, (BLOCK, 2 * BLOCK), 1) % BLOCK)
    nt = (((1,), (1,)), ((), ()))
    chains = [(j, g, half) for j in range(sub // BLOCK) for g in range(N_KV_HEADS)
              for half in range(2)]

    def scores(s, j, g, half):
        q = val["q", s]
        qrows = slice(j * BLOCK, (j + 1) * BLOCK)
        qg = jnp.concatenate([q[qrows, (2 * g) * V7X_LANES:(2 * g + 1) * V7X_LANES],
                              q[qrows, (2 * g + 1) * V7X_LANES:(2 * g + 2) * V7X_LANES]], axis=0)
        band = slice(s * sub + j * BLOCK, s * sub + (j + 2) * BLOCK)
        return lax.dot_general(kp_ref[2 * half + g, band, :], qg, nt,
                               preferred_element_type=_F32)

    def score_phase(s):
        stats = []
        for c, (j, g, half) in enumerate(chains):
            sc = scores(s, j, g, half)
            prev, cur = sc[:BLOCK], sc[BLOCK:]
            if s == 0 and j == 0:
                prev = prev + prev_mask
            sc = jnp.where(from_prev, prev, cur) + bias_ref[2 * g + half]
            s_ref[c] = sc
            sink = LOG2E * jnp.where(lane < BLOCK, sink_ref[GQA_GROUP * g + half],
                                     sink_ref[GQA_GROUP * g + 2 + half])
            stats.append((jnp.maximum(jnp.max(sc, axis=0, keepdims=True), sink), sink))
        val.pop(("q", s))
        return stats

    def value_phase(s, stats, fillers):
        outs = {}
        for c, (j, g, half) in enumerate(chains):
            if c in fillers:
                fillers[c]()
            m, sink = stats[c]
            p = jnp.exp2(s_ref[c] - m)
            p = jnp.concatenate([jnp.where(from_prev, p, 0.0), jnp.where(from_prev, 0.0, p)],
                                axis=0).astype(_BF16)
            band = slice(s * sub + j * BLOCK, s * sub + (j + 2) * BLOCK)
            o = jnp.dot(vt_ref[:, band], p, preferred_element_type=_F32)
            denom = o[KV_WIDTH:KV_WIDTH + 1, :] + jnp.exp2(sink - m)
            outs[half] = o[HEAD_DIM * g:HEAD_DIM * (g + 1), :] * (1.0 / denom)
            if half == 1:
                for pr in range(2):
                    qcols = slice(pr * BLOCK, (pr + 1) * BLOCK)
                    at = jnp.concatenate([outs[0][:, qcols], outs[1][:, qcols]], axis=0)
                    mix_ref[rows(s, j * BLOCK, (j + 1) * BLOCK),
                            (2 * g + pr) * V7X_LANES:(2 * g + pr + 1) * V7X_LANES] = (
                                at.T.astype(_BF16))

    def spread(thunks, first, last):
        return {first + (i * (last - first)) // len(thunks): t for i, t in enumerate(thunks)}

    @pl.when(pl.program_id(0) == 0)
    def _():
        conv_in(0)

    for step in projection_steps[1:]:
        step(0)
    pending = []
    for s in range(n_sub):
        stats = score_phase(s)
        own_half_done = len(chains) // 2 + 1
        if s + 1 < n_sub:
            fillers = spread(pending + [functools.partial(step, s + 1) for step in projection_steps],
                             0, len(chains))
            pending = [functools.partial(out_proj, s, 0, sub // 2),
                       functools.partial(out_proj, s, sub // 2, sub)]
        else:
            fillers = spread(pending, 0, own_half_done) if pending else {}
            fillers[own_half_done] = functools.partial(out_proj, s, 0, sub // 2)
            pending = [functools.partial(out_proj, s, sub // 2, sub)]
        value_phase(s, stats, fillers)
    conv_in(0, xnext_ref[...])
    for thunk in pending:
        thunk()

    u_ref[0:V7X_SUBLANES, :] = u_ref[tm:tm + V7X_SUBLANES, :]
    kp_ref[:, 0:BLOCK, :] = kp_ref[:, tm:tm + BLOCK, :]
    vt_ref[0:KV_WIDTH, 0:BLOCK] = vt_ref[0:KV_WIDTH, tm:tm + BLOCK]


def _mixer(x, gain, w_in, conv_w, sink, w_out, seq_len, *, layer):
    tokens = x.shape[0]
    tm = TOKEN_TILE
    sub = MIXER_SUB_TILE
    assert seq_len % tm == 0 and tm % sub == 0 and sub % BLOCK == 0
    bias = jnp.asarray(_attn_bias_table())
    n_chains = (sub // BLOCK) * N_KV_HEADS * 2
    stage_in, stage_out = _stage_shape((D_MODEL, CH_END)), _stage_shape((MIX_WIDTH, D_MODEL))
    tile_bytes = (2 * tm + sub) * D_MODEL * 4
    resident = (D_MODEL * CH_END * 2 + MIX_WIDTH * D_MODEL * 2 + bias.size * 4
                + (D_MODEL + CONV_K * CONV_WIDTH) * 4
                + 4 * (math.prod(stage_in) + math.prod(stage_out)))
    scratch = ((4 * V7X_LANES + VT_ROWS) * (BLOCK + tm) * 2 + (V7X_SUBLANES + tm) * CONV_WIDTH * 4
               + tm * MIX_WIDTH * 2 + n_chains * 2 * BLOCK * BLOCK * 4 + tm * D_MODEL * 2
               + sub * CONV_WIDTH * 4)
    temps = sub * CH_END * 4 + 8 * 1024 * 1024
    row = pl.BlockSpec((tm, D_MODEL), lambda i: (i, 0))
    return pl.pallas_call(
        functools.partial(_mixer_body, layer=layer, tiles_per_seq=seq_len // tm),
        grid=(tokens // tm,),
        in_specs=[pl.BlockSpec(memory_space=pltpu.SMEM), row, _next_first_sub_tile(tokens, tm, sub),
                  _resident((1, D_MODEL)), _HBM, _resident((CONV_K, CONV_WIDTH)), _HBM,
                  _resident(bias.shape)],
        out_specs=row,
        out_shape=jax.ShapeDtypeStruct(x.shape, _F32),
        scratch_shapes=[pltpu.VMEM((4, BLOCK + tm, V7X_LANES), _BF16),
                        pltpu.VMEM((VT_ROWS, BLOCK + tm), _BF16),
                        pltpu.VMEM((V7X_SUBLANES + tm, CONV_WIDTH), _F32),
                        pltpu.VMEM((tm, MIX_WIDTH), _BF16),
                        pltpu.VMEM((n_chains, BLOCK, 2 * BLOCK), _F32),
                        pltpu.VMEM((tm, D_MODEL), _BF16),
                        pltpu.VMEM((sub, CONV_WIDTH), _F32),
                        pltpu.VMEM((D_MODEL, CH_END), _BF16), pltpu.VMEM((MIX_WIDTH, D_MODEL), _BF16),
                        pltpu.VMEM(stage_in, _F32), pltpu.VMEM(stage_out, _F32),
                        pltpu.SemaphoreType.DMA((2, WEIGHT_STAGE_SLOTS))],
        compiler_params=pltpu.CompilerParams(
            dimension_semantics=("arbitrary",),
            vmem_limit_bytes=_vmem_limit(resident + scratch, tile_bytes, temps)),
        name="mixer",
    )(sink, x, x, gain, w_in, conv_w, w_out, bias)


def kernel(x, ffn1_norm, ffn1_wg, ffn1_wu, ffn1_wd, mix_norm, w_in, conv_w, attn_sink, w_out,
           ffn2_norm, ffn2_wg, ffn2_wu, ffn2_wd, final_norm):
    bsz, seq, d = x.shape
    depth = w_in.shape[0]
    xt = x.reshape(bsz * seq, d)
    fgain = final_norm.reshape(1, d)
    for l in range(depth):
        xt = _ffn(xt, ffn1_norm[l].reshape(1, d), ffn1_wg, ffn1_wu, ffn1_wd, fgain,
                  layer=l, final_norm=False)
        xt = _mixer(xt, mix_norm[l].reshape(1, d), w_in, conv_w[l], attn_sink[l], w_out, seq,
                    layer=l)
        xt = _ffn(xt, ffn2_norm[l].reshape(1, d), ffn2_wg, ffn2_wu, ffn2_wd, fgain,
                  layer=l, final_norm=(l == depth - 1))
    return xt.reshape(bsz, seq, d)
```

```python
import functools
import math

import numpy as np
import jax
import jax.numpy as jnp
from jax import lax
from jax.experimental import pallas as pl
from jax.experimental.pallas import tpu as pltpu

D_MODEL = 1024
HEAD_DIM = 64
N_Q_HEADS = 8
N_KV_HEADS = 2
GQA_GROUP = N_Q_HEADS // N_KV_HEADS
ATTN_WIDTH = N_Q_HEADS * HEAD_DIM
KV_WIDTH = N_KV_HEADS * HEAD_DIM
CONV_WIDTH = 512
MIX_WIDTH = ATTN_WIDTH + CONV_WIDTH
CONV_K = 3
WINDOW = 128
BLOCK = 128
D_FF = 2816
NORM_EPS = 1e-6

Q_END = ATTN_WIDTH
KV_END = Q_END + 2 * KV_WIDTH
CB_END = KV_END + CONV_WIDTH
CC_END = CB_END + CONV_WIDTH
CH_END = CC_END + CONV_WIDTH

V7X_LANES = 128
V7X_SUBLANES = 8
V7X_MXU_DIM = 256
V7X_VMEM_BYTES = 64 * 1024 * 1024

TOKEN_TILE = 1024
MIXER_SUB_TILE = 512
VT_ROWS = KV_WIDTH + 2 * V7X_SUBLANES
FFN_TOKEN_TILE = 1024
FFN_SUB_TILE = 512
FFN_OUT_BLOCK = 2 * V7X_MXU_DIM
FF_CHUNKS = (V7X_MXU_DIM,) * (D_FF // V7X_MXU_DIM)
WEIGHT_STAGE_BYTES = 1536 * 1024
WEIGHT_STAGE_SLOTS = 4
MASK_VALUE = -1e30
LOG2E = math.log2(math.e)

_F32 = jnp.float32
_BF16 = jnp.bfloat16


def _vmem_limit(resident_bytes, tile_bytes, temp_bytes):
    need = resident_bytes + 2 * tile_bytes + temp_bytes
    assert need <= V7X_VMEM_BYTES, need
    return int(need)


def _rms_norm(x, gain):
    inv = lax.rsqrt(jnp.mean(x * x, axis=-1, keepdims=True) + NORM_EPS)
    return (x * inv) * gain


def _resident(shape):
    zeros = (0,) * len(shape)
    return pl.BlockSpec(shape, lambda i: zeros, pipeline_mode=pl.Buffered(1))


_HBM = pl.BlockSpec(memory_space=pl.ANY)


def _next_first_sub_tile(tokens, tile, sub):
    last = tokens // sub - 1
    return pl.BlockSpec((sub, D_MODEL), lambda i: (jnp.minimum((i + 1) * (tile // sub), last), 0))


def _stage_shape(weight_shape):
    rows, cols = weight_shape
    bf16_rows = 2 * V7X_SUBLANES
    fits = [r for r in range(bf16_rows, WEIGHT_STAGE_BYTES // (cols * 4) + 1, bf16_rows)
            if rows % r == 0]
    return (WEIGHT_STAGE_SLOTS, max(fits), cols)


def _load_weights_bf16(jobs, stages, sem_ref):
    chunks, used = [], [0] * len(stages)
    for src, dst, si in jobs:
        slots, rows, _ = stages[si].shape
        for r0 in range(0, dst.shape[0], rows):
            chunks.append((src, dst, si, used[si] % slots, r0, rows))
            used[si] += 1

    def copy(chunk):
        src, _, si, slot, r0, rows = chunk
        return pltpu.make_async_copy(src.at[pl.ds(r0, rows), :], stages[si].at[slot],
                                     sem_ref.at[si, slot])

    ahead = min(stage.shape[0] for stage in stages) - 1
    for chunk in chunks[:ahead]:
        copy(chunk).start()
    for k, chunk in enumerate(chunks):
        if k + ahead < len(chunks):
            copy(chunks[k + ahead]).start()
        copy(chunk).wait()
        _, dst, si, slot, r0, rows = chunk
        dst[r0:r0 + rows, :] = stages[si][slot].astype(_BF16)


def _ffn_body(x_ref, xnext_ref, gain_ref, wg_hbm, wu_hbm, wd_hbm, fgain_ref, o_ref,
              h_ref, a_ref, wg_ref, wu_ref, wd_ref, stage_in_ref, stage_out_ref, sem_ref,
              *, layer, final_norm):
    subs = [slice(r0, r0 + FFN_SUB_TILE) for r0 in range(0, x_ref.shape[0], FFN_SUB_TILE)]

    def norm_into(rows, src):
        h_ref[rows, :] = _rms_norm(src, gain_ref[...]).astype(_BF16)

    @pl.when(pl.program_id(0) == 0)
    def _():
        _load_weights_bf16([(wg_hbm.at[layer], wg_ref, 0), (wu_hbm.at[layer], wu_ref, 0),
                            (wd_hbm.at[layer], wd_ref, 1)], [stage_in_ref, stage_out_ref], sem_ref)

    chunks = [(sum(FF_CHUNKS[:i]), sum(FF_CHUNKS[:i + 1])) for i in range(len(FF_CHUNKS))]
    out_blocks = list(range(0, D_MODEL, FFN_OUT_BLOCK))

    def up_item(si, ci):
        rows, (c0, c1) = subs[si], chunks[ci]
        if ci == 0 and si > 0:
            norm_into(rows, x_ref[rows, :])
        h = h_ref[rows, :]
        g = jnp.dot(h, wg_ref[:, c0:c1], preferred_element_type=_F32)
        u = jnp.dot(h, wu_ref[:, c0:c1], preferred_element_type=_F32)
        a_ref[rows, c0:c1] = (g / (1.0 + jnp.exp(-g)) * u).astype(_BF16)

    def first_item_of_next_step():
        norm_into(subs[0], xnext_ref[...])
        up_item(0, 0)

    done = {}

    def down_item(si, n0):
        rows, cols = subs[si], slice(n0, n0 + FFN_OUT_BLOCK)
        y = x_ref[rows, cols] + 0.5 * jnp.dot(a_ref[rows, :], wd_ref[:, cols],
                                              preferred_element_type=_F32)
        if not final_norm:
            o_ref[rows, cols] = y
            return
        done.setdefault(si, []).append(y)
        if n0 == out_blocks[-1]:
            o_ref[rows, :] = _rms_norm(jnp.concatenate(done.pop(si), axis=1), fgain_ref[...])

    @pl.when(pl.program_id(0) == 0)
    def _():
        norm_into(subs[0], x_ref[subs[0], :])
        up_item(0, 0)

    ups = [[functools.partial(up_item, si, ci) for ci in range(1 if si == 0 else 0, len(chunks))]
           for si in range(len(subs))]
    downs = [[functools.partial(down_item, si, n0) for n0 in out_blocks] for si in range(len(subs))]
    every = -(-len(chunks) // len(out_blocks))
    for si in range(len(subs)):
        for k, up in enumerate(ups[si]):
            if si > 0 and k % every == 0 and downs[si - 1]:
                downs[si - 1].pop(0)()
            up()
        if si > 0:
            for down in downs[si - 1]:
                down()
    first_item_of_next_step()
    for down in downs[-1]:
        down()


def _ffn(x, gain, wg, wu, wd, fgain, *, layer, final_norm):
    tokens = x.shape[0]
    tm = FFN_TOKEN_TILE
    assert tokens % tm == 0 and tm % FFN_SUB_TILE == 0 and sum(FF_CHUNKS) == D_FF
    stage_in, stage_out = _stage_shape((D_MODEL, D_FF)), _stage_shape((D_FF, D_MODEL))
    tile_bytes = (2 * tm + FFN_SUB_TILE) * D_MODEL * 4
    resident = (3 * D_MODEL * D_FF * 2 + 2 * D_MODEL * 4
                + 4 * (math.prod(stage_in) + math.prod(stage_out)))
    temps = (tm * (D_MODEL + D_FF) * 2 + 2 * 3 * FFN_SUB_TILE * max(FF_CHUNKS) * 4
             + 2 * FFN_SUB_TILE * D_MODEL * 4)
    row = pl.BlockSpec((tm, D_MODEL), lambda i: (i, 0))
    return pl.pallas_call(
        functools.partial(_ffn_body, layer=layer, final_norm=final_norm),
        grid=(tokens // tm,),
        in_specs=[row, _next_first_sub_tile(tokens, tm, FFN_SUB_TILE), _resident((1, D_MODEL)),
                  _HBM, _HBM, _HBM, _resident((1, D_MODEL))],
        out_specs=row,
        out_shape=jax.ShapeDtypeStruct(x.shape, _F32),
        scratch_shapes=[pltpu.VMEM((tm, D_MODEL), _BF16), pltpu.VMEM((tm, D_FF), _BF16),
                        pltpu.VMEM((D_MODEL, D_FF), _BF16), pltpu.VMEM((D_MODEL, D_FF), _BF16),
                        pltpu.VMEM((D_FF, D_MODEL), _BF16),
                        pltpu.VMEM(stage_in, _F32), pltpu.VMEM(stage_out, _F32),
                        pltpu.SemaphoreType.DMA((2, WEIGHT_STAGE_SLOTS))],
        compiler_params=pltpu.CompilerParams(
            dimension_semantics=("arbitrary",),
            vmem_limit_bytes=_vmem_limit(resident, tile_bytes, temps)),
        name="ffn_final" if final_norm else "ffn",
    )(x, x, gain, wg, wu, wd, fgain)


def _attn_bias_table():
    assert WINDOW == BLOCK
    r = np.arange(BLOCK)[:, None]
    qi = np.arange(BLOCK)[None, :]
    dist = ((qi - r) % BLOCK).astype(np.float32)
    slopes = np.exp2(-8.0 * np.arange(1, N_Q_HEADS + 1, dtype=np.float32) / N_Q_HEADS)
    per_head = -slopes[:, None, None] * dist[None] * np.float32(LOG2E)
    table = [np.concatenate([per_head[GQA_GROUP * g + half], per_head[GQA_GROUP * g + 2 + half]], axis=1)
             for g in range(N_KV_HEADS) for half in range(2)]
    return np.stack(table).astype(np.float32)


def _place_heads(t):
    lo = lax.broadcasted_iota(jnp.int32, t.shape, 1) < HEAD_DIM
    r = pltpu.roll(t, HEAD_DIM, axis=1)
    zero = jnp.zeros_like(t)
    placed = (jnp.where(lo, t, zero), jnp.where(lo, r, zero),
              jnp.where(lo, zero, r), jnp.where(lo, zero, t))
    return [p.astype(_BF16) for p in placed]


def _mixer_body(sink_ref, x_ref, xnext_ref, gain_ref, win_hbm, convw_ref, wout_hbm, bias_ref,
                o_ref, kp_ref, vt_ref, u_ref, mix_ref, s_ref, h_ref, cc_ref,
                win_ref, wout_ref, stage_in_ref, stage_out_ref, sem_ref, *, layer, tiles_per_seq):
    tm = x_ref.shape[0]
    first = pl.program_id(0) % tiles_per_seq == 0

    @pl.when(pl.program_id(0) == 0)
    def _():
        _load_weights_bf16([(win_hbm.at[layer], win_ref, 0), (wout_hbm.at[layer], wout_ref, 1)],
                           [stage_in_ref, stage_out_ref], sem_ref)
        pad_rows = lax.broadcasted_iota(jnp.int32, (VT_ROWS - KV_WIDTH, vt_ref.shape[1]), 0)
        vt_ref[KV_WIDTH:, :] = jnp.where(pad_rows == 0, 1.0, 0.0).astype(_BF16)

    @pl.when(first)
    def _():
        kp_ref[:, 0:BLOCK, :] = jnp.zeros((4, BLOCK, V7X_LANES), _BF16)
        vt_ref[0:KV_WIDTH, 0:BLOCK] = jnp.zeros((KV_WIDTH, BLOCK), _BF16)
        u_ref[0:V7X_SUBLANES, :] = jnp.zeros((V7X_SUBLANES, CONV_WIDTH), _F32)

    sub = MIXER_SUB_TILE
    n_sub = tm // sub
    val = {}

    def rows(s, lo=0, hi=sub):
        return slice(s * sub + lo, s * sub + hi)

    def proj(s, c0, c1):
        return jnp.dot(h_ref[rows(s), :], win_ref[:, c0:c1], preferred_element_type=_F32)

    def conv_in(s, src=None):
        src = x_ref[rows(s), :] if src is None else src
        h_ref[rows(s), :] = _rms_norm(src, gain_ref[...]).astype(_BF16)
        if s == 0:
            cc_ref[...] = proj(s, CB_END, CC_END)
        else:
            val["cc", s] = proj(s, CB_END, CC_END)

    def conv_taps(s):
        u = (cc_ref[...] if s == 0 else val.pop(("cc", s))) * proj(s, CC_END, CH_END)
        base = V7X_SUBLANES + s * sub
        u_ref[base:base + sub, :] = u
        prev = u_ref[base - V7X_SUBLANES:base, :]
        row = lax.broadcasted_iota(jnp.int32, prev.shape, 0)

        def shifted(k):
            rolled = pltpu.roll(u, k, axis=0)
            head = jnp.where(row < k, pltpu.roll(prev, k, axis=0), rolled[0:V7X_SUBLANES, :])
            return jnp.concatenate([head, rolled[V7X_SUBLANES:, :]], axis=0)

        cw = convw_ref[...]
        val["y", s] = cw[0:1, :] * shifted(2) + cw[1:2, :] * shifted(1) + cw[2:3, :] * u

    def step_q(s):
        val["q", s] = (proj(s, 0, Q_END) * (LOG2E / math.sqrt(HEAD_DIM))).astype(_BF16)

    def step_kv(s):
        kv = proj(s, Q_END, KV_END)
        band = slice(BLOCK + s * sub, BLOCK + (s + 1) * sub)
        for idx, p in enumerate(_place_heads(kv[:, :KV_WIDTH])):
            kp_ref[idx, band, :] = p
        vt_ref[0:KV_WIDTH, band] = kv[:, KV_WIDTH:].T.astype(_BF16)

    def conv_gate(s):
        mix_ref[rows(s), ATTN_WIDTH:] = (proj(s, KV_END, CB_END) * val.pop(("y", s))).astype(_BF16)

    projection_steps = (conv_in, conv_taps, step_q, step_kv, conv_gate)

    def out_proj(s, lo, hi):
        r = rows(s, lo, hi)
        o_ref[r, :] = x_ref[r, :] + jnp.dot(mix_ref[r, :], wout_ref[...],
                                            preferred_element_type=_F32)

    prev_mask = jnp.where(first, MASK_VALUE, 0.0).astype(_F32)
    lane = lax.broadcasted_iota(jnp.int32, (1, 2 * BLOCK), 1)
    from_prev = (lax.broadcasted_iota(jnp.int32, (BLOCK, 2 * BLOCK), 0)
                 > lax.broadcasted_iota(jnp.int32, (BLOCK, 2 * BLOCK), 1) % BLOCK)
    nt = (((1,), (1,)), ((), ()))
    chains = [(j, g, half) for j in range(sub // BLOCK) for g in range(N_KV_HEADS)
              for half in range(2)]

    def scores(s, j, g, half):
        q = val["q", s]
        qrows = slice(j * BLOCK, (j + 1) * BLOCK)
        qg = jnp.concatenate([q[qrows, (2 * g) * V7X_LANES:(2 * g + 1) * V7X_LANES],
                              q[qrows, (2 * g + 1) * V7X_LANES:(2 * g + 2) * V7X_LANES]], axis=0)
        band = slice(s * sub + j * BLOCK, s * sub + (j + 2) * BLOCK)
        return lax.dot_general(kp_ref[2 * half + g, band, :], qg, nt,
                               preferred_element_type=_F32)

    def score_phase(s):
        stats = []
        for c, (j, g, half) in enumerate(chains):
            sc = scores(s, j, g, half)
            prev, cur = sc[:BLOCK], sc[BLOCK:]
            if s == 0 and j == 0:
                prev = prev + prev_mask
            sc = jnp.where(from_prev, prev, cur) + bias_ref[2 * g + half]
            s_ref[c] = sc
            sink = LOG2E * jnp.where(lane < BLOCK, sink_ref[GQA_GROUP * g + half],
                                     sink_ref[GQA_GROUP * g + 2 + half])
            stats.append((jnp.maximum(jnp.max(sc, axis=0, keepdims=True), sink), sink))
        val.pop(("q", s))
        return stats

    def value_phase(s, stats, fillers):
        outs = {}
        for c, (j, g, half) in enumerate(chains):
            if c in fillers:
                fillers[c]()
            m, sink = stats[c]
            p = jnp.exp2(s_ref[c] - m)
            p = jnp.concatenate([jnp.where(from_prev, p, 0.0), jnp.where(from_prev, 0.0, p)],
                                axis=0).astype(_BF16)
            band = slice(s * sub + j * BLOCK, s * sub + (j + 2) * BLOCK)
            o = jnp.dot(vt_ref[:, band], p, preferred_element_type=_F32)
            denom = o[KV_WIDTH:KV_WIDTH + 1, :] + jnp.exp2(sink - m)
            outs[half] = o[HEAD_DIM * g:HEAD_DIM * (g + 1), :] * (1.0 / denom)
            if half == 1:
                for pr in range(2):
                    qcols = slice(pr * BLOCK, (pr + 1) * BLOCK)
                    at = jnp.concatenate([outs[0][:, qcols], outs[1][:, qcols]], axis=0)
                    mix_ref[rows(s, j * BLOCK, (j + 1) * BLOCK),
                            (2 * g + pr) * V7X_LANES:(2 * g + pr + 1) * V7X_LANES] = (
                                at.T.astype(_BF16))

    def spread(thunks, first, last):
        return {first + (i * (last - first)) // len(thunks): t for i, t in enumerate(thunks)}

    @pl.when(pl.program_id(0) == 0)
    def _():
        conv_in(0)

    for step in projection_steps[1:]:
        step(0)
    pending = []
    for s in range(n_sub):
        stats = score_phase(s)
        own_half_done = len(chains) // 2 + 1
        if s + 1 < n_sub:
            fillers = spread(pending + [functools.partial(step, s + 1) for step in projection_steps],
                             0, len(chains))
            pending = [functools.partial(out_proj, s, 0, sub // 2),
                       functools.partial(out_proj, s, sub // 2, sub)]
        else:
            fillers = spread(pending, 0, own_half_done) if pending else {}
            fillers[own_half_done] = functools.partial(out_proj, s, 0, sub // 2)
            pending = [functools.partial(out_proj, s, sub // 2, sub)]
        value_phase(s, stats, fillers)
    conv_in(0, xnext_ref[...])
    for thunk in pending:
        thunk()

    u_ref[0:V7X_SUBLANES, :] = u_ref[tm:tm + V7X_SUBLANES, :]
    kp_ref[:, 0:BLOCK, :] = kp_ref[:, tm:tm + BLOCK, :]
    vt_ref[0:KV_WIDTH, 0:BLOCK] = vt_ref[0:KV_WIDTH, tm:tm + BLOCK]


def _mixer(x, gain, w_in, conv_w, sink, w_out, seq_len, *, layer):
    tokens = x.shape[0]
    tm = TOKEN_TILE
    sub = MIXER_SUB_TILE
    assert seq_len % tm == 0 and tm % sub == 0 and sub % BLOCK == 0
    bias = jnp.asarray(_attn_bias_table())
    n_chains = (sub // BLOCK) * N_KV_HEADS * 2
    stage_in, stage_out = _stage_shape((D_MODEL, CH_END)), _stage_shape((MIX_WIDTH, D_MODEL))
    tile_bytes = (2 * tm + sub) * D_MODEL * 4
    resident = (D_MODEL * CH_END * 2 + MIX_WIDTH * D_MODEL * 2 + bias.size * 4
                + (D_MODEL + CONV_K * CONV_WIDTH) * 4
                + 4 * (math.prod(stage_in) + math.prod(stage_out)))
    scratch = ((4 * V7X_LANES + VT_ROWS) * (BLOCK + tm) * 2 + (V7X_SUBLANES + tm) * CONV_WIDTH * 4
               + tm * MIX_WIDTH * 2 + n_chains * 2 * BLOCK * BLOCK * 4 + tm * D_MODEL * 2
               + sub * CONV_WIDTH * 4)
    temps = sub * CH_END * 4 + 8 * 1024 * 1024
    row = pl.BlockSpec((tm, D_MODEL), lambda i: (i, 0))
    return pl.pallas_call(
        functools.partial(_mixer_body, layer=layer, tiles_per_seq=seq_len // tm),
        grid=(tokens // tm,),
        in_specs=[pl.BlockSpec(memory_space=pltpu.SMEM), row, _next_first_sub_tile(tokens, tm, sub),
                  _resident((1, D_MODEL)), _HBM, _resident((CONV_K, CONV_WIDTH)), _HBM,
                  _resident(bias.shape)],
        out_specs=row,
        out_shape=jax.ShapeDtypeStruct(x.shape, _F32),
        scratch_shapes=[pltpu.VMEM((4, BLOCK + tm, V7X_LANES), _BF16),
                        pltpu.VMEM((VT_ROWS, BLOCK + tm), _BF16),
                        pltpu.VMEM((V7X_SUBLANES + tm, CONV_WIDTH), _F32),
                        pltpu.VMEM((tm, MIX_WIDTH), _BF16),
                        pltpu.VMEM((n_chains, BLOCK, 2 * BLOCK), _F32),
                        pltpu.VMEM((tm, D_MODEL), _BF16),
                        pltpu.VMEM((sub, CONV_WIDTH), _F32),
                        pltpu.VMEM((D_MODEL, CH_END), _BF16), pltpu.VMEM((MIX_WIDTH, D_MODEL), _BF16),
                        pltpu.VMEM(stage_in, _F32), pltpu.VMEM(stage_out, _F32),
                        pltpu.SemaphoreType.DMA((2, WEIGHT_STAGE_SLOTS))],
        compiler_params=pltpu.CompilerParams(
            dimension_semantics=("arbitrary",),
            vmem_limit_bytes=_vmem_limit(resident + scratch, tile_bytes, temps)),
        name="mixer",
    )(sink, x, x, gain, w_in, conv_w, w_out, bias)


def kernel(x, ffn1_norm, ffn1_wg, ffn1_wu, ffn1_wd, mix_norm, w_in, conv_w, attn_sink, w_out,
           ffn2_norm, ffn2_wg, ffn2_wu, ffn2_wd, final_norm):
    bsz, seq, d = x.shape
    depth = w_in.shape[0]
    xt = x.reshape(bsz * seq, d)
    fgain = final_norm.reshape(1, d)
    for l in range(depth):
        xt = _ffn(xt, ffn1_norm[l].reshape(1, d), ffn1_wg, ffn1_wu, ffn1_wd, fgain,
                  layer=l, final_norm=False)
        xt = _mixer(xt, mix_norm[l].reshape(1, d), w_in, conv_w[l], attn_sink[l], w_out, seq,
                    layer=l)
        xt = _ffn(xt, ffn2_norm[l].reshape(1, d), ffn2_wg, ffn2_wu, ffn2_wd, fgain,
                  layer=l, final_norm=(l == depth - 1))
    return xt.reshape(bsz, seq, d)
```
